```python
import jax, jax.numpy as jnp
from jax import lax
import numpy as np

D_MODEL = 2048
BATCH = 1
SEQ = 16384
DEPTH = 1
DEC_BATCH = 128
DEC_SEQ = 1
PAST_LEN = 16384
PAGE_SIZE = 128

MIX_WIDTH = D_MODEL
ATTN_WIDTH = MIX_WIDTH // 2
GLA_WIDTH = MIX_WIDTH - ATTN_WIDTH
HEAD_DIM = 64
N_HEADS = ATTN_WIDTH // HEAD_DIM
N_KV_HEADS = 4
GQA_GROUP = N_HEADS // N_KV_HEADS
WINDOW = 128
ATTN_BLOCK = WINDOW
ROPE_DIM = HEAD_DIM // 4
ROPE_THETA = 500000.0
GLA_HEADS = 4
GLA_DV = GLA_WIDTH // GLA_HEADS
GLA_DK = GLA_DV // 2
GLA_RANK = 16
GLA_TAU = 16.0
GLA_CHUNK = 64
GLA_CHUNK_STEP = 16
Q_COLS = N_HEADS * HEAD_DIM
KV_COLS = N_KV_HEADS * HEAD_DIM
GLA_QK_COLS = GLA_HEADS * GLA_DK
IN_COLS = Q_COLS + 2 * KV_COLS + 2 * GLA_QK_COLS + 2 * GLA_WIDTH + GLA_RANK
N_EXPERTS = 256
TOP_K = 8
N_GROUPS = 8
TOPK_GROUPS = 4
EXPERT_FF = D_MODEL // 4
SHARED_FF = EXPERT_FF
ROUTED_SCALE = 2.5
DEEPNORM_ALPHA = (2 * DEPTH) ** 0.25
DEEPNORM_BETA = (8 * DEPTH) ** -0.25
LN_EPS = 1e-5
RMS_EPS = 1e-6

kernel_name = 'hymba_swa_sink_gla_moe_deepnorm_step'


def layernorm(x, g, b):
    xf = x.astype(jnp.float32)
    mu = xf.mean(-1, keepdims=True)
    var = jnp.mean(jnp.square(xf - mu), -1, keepdims=True)
    return ((xf - mu) * lax.rsqrt(var + LN_EPS) * g.astype(jnp.float32) + b.astype(jnp.float32)).astype(x.dtype)


def rmsnorm(x, g):
    xf = x.astype(jnp.float32)
    y = xf * lax.rsqrt(jnp.mean(jnp.square(xf), -1, keepdims=True) + RMS_EPS)
    return (y * g.astype(jnp.float32)).astype(x.dtype)


def partial_rope(x, pos):
    half = ROPE_DIM // 2
    inv = 1.0 / (ROPE_THETA ** (jnp.arange(0, ROPE_DIM, 2, dtype=jnp.float32) / ROPE_DIM))
    ang = pos.astype(jnp.float32)[:, None] * inv[None, :]
    cos, sin = jnp.cos(ang)[:, None, :], jnp.sin(ang)[:, None, :]
    xr = x[..., :ROPE_DIM].astype(jnp.float32)
    x1, x2 = xr[..., :half], xr[..., half:]
    rot = jnp.concatenate([x1 * cos - x2 * sin, x2 * cos + x1 * sin], axis=-1)
    return jnp.concatenate([rot.astype(x.dtype), x[..., ROPE_DIM:]], axis=-1)


def sink_softmax(s, sinks):
    sk = sinks.astype(jnp.float32).reshape(N_KV_HEADS, GQA_GROUP, 1, 1)
    m = jnp.maximum(s.max(-1, keepdims=True), sk)
    p = jnp.exp(s - m)
    return p / (p.sum(-1, keepdims=True) + jnp.exp(sk - m))


def swa_banded(q, k, v, sinks):
    B, L = q.shape[:2]
    nb = L // ATTN_BLOCK
    qb = q.reshape(B, nb, ATTN_BLOCK, N_KV_HEADS, GQA_GROUP, HEAD_DIM)

    def band(t):
        tb = t.reshape(B, nb, ATTN_BLOCK, N_KV_HEADS, HEAD_DIM)
        prev = jnp.concatenate([jnp.zeros_like(tb[:, :1]), tb[:, :-1]], axis=1)
        return jnp.concatenate([prev, tb], axis=2)

    kb, vb = band(k), band(v)
    qi = jnp.arange(ATTN_BLOCK) + ATTN_BLOCK
    kj = jnp.arange(2 * ATTN_BLOCK)
    diff = qi[:, None] - kj[None, :]
    in_window = (diff >= 0) & (diff < WINDOW)
    has_prev = (jnp.arange(nb) > 0)[:, None, None] | (kj >= ATTN_BLOCK)[None, None, :]
    mask = in_window[None] & has_prev
    s = jnp.einsum('bnqkgd,bnskd->bnkgqs', qb, kb).astype(jnp.float32) * (HEAD_DIM ** -0.5)
    s = jnp.where(mask[None, :, None, None], s, -jnp.inf)
    p = sink_softmax(s, sinks)
    o = jnp.einsum('bnkgqs,bnskd->bnqkgd', p.astype(vb.dtype), vb)
    return o.reshape(B, L, N_HEADS * HEAD_DIM)


def swa_step(q, k, v, k_buf, v_buf, sinks):
    B, T = q.shape[:2]
    Lb = k_buf.shape[1]
    keys = jnp.concatenate([k_buf, k.astype(k_buf.dtype)], axis=1)
    vals = jnp.concatenate([v_buf, v.astype(v_buf.dtype)], axis=1)
    qpos = PAST_LEN + jnp.arange(T)
    kpos = PAST_LEN - Lb + jnp.arange(Lb + T)
    diff = qpos[:, None] - kpos[None, :]
    mask = (diff >= 0) & (diff < WINDOW)
    qg = q.reshape(B, T, N_KV_HEADS, GQA_GROUP, HEAD_DIM)
    s = jnp.einsum('btkgd,bskd->bkgts', qg, keys.astype(q.dtype)).astype(jnp.float32) * (HEAD_DIM ** -0.5)
    s = jnp.where(mask, s, -jnp.inf)
    p = sink_softmax(s, sinks)
    o = jnp.einsum('bkgts,bskd->btkgd', p.astype(q.dtype), vals.astype(q.dtype))
    return o.reshape(B, T, N_HEADS * HEAD_DIM), keys[:, T:], vals[:, T:]


def largest_divisor_at_most(n, cap):
    c = min(n, cap)
    while n % c:
        c -= 1
    return c


def gla_chunked(q, k, v, log_a, s0, chunk):
    B, L, H, DK = q.shape
    DV = v.shape[-1]
    n = L // chunk

    def chunks(t):
        return jnp.moveaxis(t.reshape(B, n, chunk, H, t.shape[-1]), 1, 0)

    causal = jnp.tril(jnp.ones((chunk, chunk), dtype=bool))[None, :, :, None, None]

    def step(s, xs):
        qc, kc, vc, ac = xs
        b = jnp.cumsum(ac, axis=1)
        decay = jnp.exp(jnp.where(causal, b[:, :, None] - b[:, None, :], -jnp.inf))
        scores = jnp.einsum('bthd,btshd,bshd->bhts', qc, decay, kc)
        o = jnp.einsum('bhts,bshv->bthv', scores, vc) + jnp.einsum('bthd,bhdv->bthv', qc * jnp.exp(b), s)
        b_last = b[:, -1]
        s_new = jnp.exp(b_last)[..., None] * s + jnp.einsum('bshd,bshv->bhdv', kc * jnp.exp(b_last[:, None] - b), vc)
        return s_new, o

    s_fin, o = lax.scan(step, s0, (chunks(q), chunks(k), chunks(v), chunks(log_a)))
    return jnp.moveaxis(o, 0, 1).reshape(B, L, H, DV), s_fin


def token_mixer(x, pos, k_buf, v_buf, s0, w_in, attn_sinks, attn_norm_g, w_gla_a2, b_gla_a, gla_norm_g, w_out, prompt):
    B, L, _ = x.shape
    sizes = [Q_COLS, KV_COLS, KV_COLS, GLA_QK_COLS, GLA_QK_COLS, GLA_WIDTH, GLA_WIDTH, GLA_RANK]
    cuts = np.cumsum(sizes)[:-1].tolist()
    q, k, v, qg, kg, vg, rg, ag = jnp.split(x @ w_in, cuts, axis=-1)
    q = partial_rope(q.reshape(B, L, N_HEADS, HEAD_DIM), pos)
    k = partial_rope(k.reshape(B, L, N_KV_HEADS, HEAD_DIM), pos)
    v = v.reshape(B, L, N_KV_HEADS, HEAD_DIM)
    if prompt:
        o_attn = swa_banded(q, k, v, attn_sinks)
        keep = min(WINDOW, L)
        new_k, new_v = k[:, L - keep:], v[:, L - keep:]
    else:
        o_attn, new_k, new_v = swa_step(q, k, v, k_buf, v_buf, attn_sinks)
    o_attn = rmsnorm(o_attn, attn_norm_g)
    f32 = jnp.float32
    qg = qg.reshape(B, L, GLA_HEADS, GLA_DK).astype(f32) * (GLA_DK ** -0.5)
    kg = kg.reshape(B, L, GLA_HEADS, GLA_DK).astype(f32)
    vg = vg.reshape(B, L, GLA_HEADS, GLA_DV).astype(f32)
    log_a = jax.nn.log_sigmoid((ag @ w_gla_a2 + b_gla_a).astype(f32)) / GLA_TAU
    log_a = log_a.reshape(B, L, GLA_HEADS, GLA_DK)
    if prompt:
        s_init = jnp.zeros((B, GLA_HEADS, GLA_DK, GLA_DV), f32)
        chunk = largest_divisor_at_most(L, GLA_CHUNK)
    else:
        s_init = s0.astype(f32)
        chunk = largest_divisor_at_most(L, GLA_CHUNK_STEP)
    o_gla, new_s = gla_chunked(qg, kg, vg, log_a, s_init, chunk)
    o_gla = rmsnorm(o_gla, gla_norm_g.reshape(GLA_HEADS, GLA_DV)).reshape(B, L, GLA_WIDTH)
    o_gla = o_gla.astype(x.dtype) * jax.nn.silu(rg)
    y = jnp.concatenate([o_attn, o_gla], axis=-1) @ w_out
    new_s = new_s.astype(x.dtype) if s0 is None else new_s.astype(s0.dtype)
    return y, new_k, new_v, new_s


def moe_block_size(n_assign):
    blk = 8
    while blk < 128 and blk * N_EXPERTS < n_assign:
        blk *= 2
    return blk


def routed_experts(xt, w_router, router_bias, w_exp_gate, w_exp_up, w_exp_down):
    T, D = xt.shape
    scores = jax.nn.sigmoid(xt.astype(jnp.float32) @ w_router.astype(jnp.float32))
    biased = scores + router_bias.astype(jnp.float32)
    per_group = N_EXPERTS // N_GROUPS
    grp_score = lax.top_k(biased.reshape(T, N_GROUPS, per_group), 2)[0].sum(-1)
    _, gidx = lax.top_k(grp_score, TOPK_GROUPS)
    gmask = jax.nn.one_hot(gidx, N_GROUPS, dtype=jnp.float32).sum(1) > 0
    masked = jnp.where(jnp.repeat(gmask, per_group, axis=1), biased, -jnp.inf)
    _, eidx = lax.top_k(masked, TOP_K)
    gates = jnp.take_along_axis(scores, eidx, axis=1)
    gates = gates / gates.sum(-1, keepdims=True) * ROUTED_SCALE
    n = T * TOP_K
    blk = moe_block_size(n)
    nb = -(-(n + N_EXPERTS * (blk - 1)) // blk)
    flat_e = eidx.reshape(-1).astype(jnp.int32)
    flat_tok = jnp.arange(n, dtype=jnp.int32) // TOP_K
    flat_g = gates.reshape(-1)
    counts = jnp.bincount(flat_e, length=N_EXPERTS).astype(jnp.int32)
    starts = jnp.cumsum(counts) - counts
    padded = (counts + blk - 1) // blk * blk
    pend = jnp.cumsum(padded)
    pstart = pend - padded
    order = jnp.argsort(flat_e)
    se = flat_e[order]
    slot = pstart[se] + jnp.arange(n, dtype=jnp.int32) - starts[se]
    rows_tok = jnp.full((nb * blk,), T, jnp.int32).at[slot].set(flat_tok[order])
    rows_gate = jnp.zeros((nb * blk,), jnp.float32).at[slot].set(flat_g[order])
    block_expert = jnp.minimum(jnp.searchsorted(pend, jnp.arange(nb, dtype=jnp.int32) * blk, side='right'), N_EXPERTS - 1)
    xpad = jnp.concatenate([xt, jnp.zeros((1, D), xt.dtype)], axis=0)

    def expert_block(args):
        rows, e = args
        xb = xpad[rows]
        h = jax.nn.silu(xb @ w_exp_gate[e]) * (xb @ w_exp_up[e])
        return h @ w_exp_down[e]

    yb = lax.map(expert_block, (rows_tok.reshape(nb, blk), block_expert))
    yb = yb.reshape(nb * blk, D) * rows_gate[:, None].astype(yb.dtype)
    return jax.ops.segment_sum(yb, rows_tok, num_segments=T + 1)[:T]


def shared_expert(xt, w_sh_gate, w_sh_up, w_sh_down):
    return (jax.nn.silu(xt @ w_sh_gate) * (xt @ w_sh_up)) @ w_sh_down


def trunk_layer(x, pos, k_buf, v_buf, s0, w_in, attn_sinks, attn_norm_g, w_gla_a2, b_gla_a, gla_norm_g, w_out,
                ln1_g, ln1_b, w_router, router_bias, w_exp_gate, w_exp_up, w_exp_down,
                w_sh_gate, w_sh_up, w_sh_down, ln2_g, ln2_b, prompt):
    mix, nk, nv, ns = token_mixer(x, pos, k_buf, v_buf, s0, w_in, attn_sinks, attn_norm_g, w_gla_a2, b_gla_a,
                                  gla_norm_g, w_out, prompt)
    x = layernorm(DEEPNORM_ALPHA * x + mix, ln1_g, ln1_b)
    B, L, D = x.shape
    xt = x.reshape(B * L, D)
    f = shared_expert(xt, w_sh_gate, w_sh_up, w_sh_down) + routed_experts(xt, w_router, router_bias, w_exp_gate, w_exp_up, w_exp_down)
    x = layernorm(DEEPNORM_ALPHA * x + f.reshape(B, L, D), ln2_g, ln2_b)
    return x, nk, nv, ns


def setup_inputs(seed: int = 0) -> dict:
    key = jax.random.key(seed)
    ks = jax.random.split(key, 24)
    nrm = lambda k, shape, scale: jax.random.normal(k, shape, jnp.float32) * scale
    win_buf = min(WINDOW, PAST_LEN)
    col_scale = jnp.concatenate([
        jnp.ones((Q_COLS + KV_COLS,), jnp.float32),
        jnp.full((KV_COLS,), DEEPNORM_BETA, jnp.float32),
        jnp.ones((2 * GLA_QK_COLS,), jnp.float32),
        jnp.full((GLA_WIDTH,), DEEPNORM_BETA, jnp.float32),
        jnp.ones((GLA_WIDTH + GLA_RANK,), jnp.float32)])
    return {
        'x_prompt': nrm(ks[0], (BATCH, SEQ, D_MODEL), 1.0),
        'x_sample': nrm(ks[1], (DEC_BATCH, DEC_SEQ, D_MODEL), 1.0),
        'cache_k_win': nrm(ks[2], (DEPTH, DEC_BATCH, win_buf, N_KV_HEADS, HEAD_DIM), 1.0),
        'cache_v_win': nrm(ks[3], (DEPTH, DEC_BATCH, win_buf, N_KV_HEADS, HEAD_DIM), 1.0),
        'state_gla': nrm(ks[4], (DEPTH, DEC_BATCH, GLA_HEADS, GLA_DK, GLA_DV), 0.1),
        'w_in': nrm(ks[5], (DEPTH, D_MODEL, IN_COLS), D_MODEL ** -0.5) * col_scale,
        'attn_sinks': nrm(ks[6], (DEPTH, N_HEADS), 0.5),
        'attn_norm_g': 1.0 + nrm(ks[7], (DEPTH, ATTN_WIDTH), 0.02),
        'w_gla_a2': nrm(ks[8], (DEPTH, GLA_RANK, GLA_QK_COLS), GLA_RANK ** -0.5),
        'b_gla_a': nrm(ks[9], (DEPTH, GLA_QK_COLS), 0.1),
        'gla_norm_g': 1.0 + nrm(ks[10], (DEPTH, GLA_WIDTH), 0.02),
        'w_out': nrm(ks[11], (DEPTH, MIX_WIDTH, D_MODEL), MIX_WIDTH ** -0.5 * DEEPNORM_BETA),
        'ln1_g': 1.0 + nrm(ks[12], (DEPTH, D_MODEL), 0.02),
        'ln1_b': nrm(ks[13], (DEPTH, D_MODEL), 0.02),
        'w_router': nrm(ks[14], (DEPTH, D_MODEL, N_EXPERTS), D_MODEL ** -0.5),
        'router_bias': nrm(ks[15], (DEPTH, N_EXPERTS), 0.01),
        'w_exp_gate': nrm(ks[16], (DEPTH, N_EXPERTS, D_MODEL, EXPERT_FF), D_MODEL ** -0.5),
        'w_exp_up': nrm(ks[17], (DEPTH, N_EXPERTS, D_MODEL, EXPERT_FF), D_MODEL ** -0.5),
        'w_exp_down': nrm(ks[18], (DEPTH, N_EXPERTS, EXPERT_FF, D_MODEL), EXPERT_FF ** -0.5 * DEEPNORM_BETA),
        'w_sh_gate': nrm(ks[19], (DEPTH, D_MODEL, SHARED_FF), D_MODEL ** -0.5),
        'w_sh_up': nrm(ks[20], (DEPTH, D_MODEL, SHARED_FF), D_MODEL ** -0.5),
        'w_sh_down': nrm(ks[21], (DEPTH, SHARED_FF, D_MODEL), SHARED_FF ** -0.5 * DEEPNORM_BETA),
        'ln2_g': 1.0 + nrm(ks[22], (DEPTH, D_MODEL), 0.02),
        'ln2_b': nrm(ks[23], (DEPTH, D_MODEL), 0.02),
    }


def reference(x_prompt, x_sample, cache_k_win, cache_v_win, state_gla, w_in, attn_sinks, attn_norm_g,
              w_gla_a2, b_gla_a, gla_norm_g, w_out, ln1_g, ln1_b, w_router, router_bias,
              w_exp_gate, w_exp_up, w_exp_down, w_sh_gate, w_sh_up, w_sh_down, ln2_g, ln2_b):
    pos_prompt = jnp.arange(x_prompt.shape[1], dtype=jnp.int32)
    pos_sample = PAST_LEN + jnp.arange(x_sample.shape[1], dtype=jnp.int32)
    yp, ys = x_prompt, x_sample
    kp, vp, sp, kq, vq, sq = [], [], [], [], [], []
    for l in range(DEPTH):
        yp, k1, v1, s1 = trunk_layer(yp, pos_prompt, None, None, None, w_in[l], attn_sinks[l], attn_norm_g[l],
                                     w_gla_a2[l], b_gla_a[l], gla_norm_g[l], w_out[l], ln1_g[l], ln1_b[l],
                                     w_router[l], router_bias[l], w_exp_gate[l], w_exp_up[l], w_exp_down[l],
                                     w_sh_gate[l], w_sh_up[l], w_sh_down[l], ln2_g[l], ln2_b[l], prompt=True)
        ys, k2, v2, s2 = trunk_layer(ys, pos_sample, cache_k_win[l], cache_v_win[l], state_gla[l], w_in[l],
                                     attn_sinks[l], attn_norm_g[l], w_gla_a2[l], b_gla_a[l], gla_norm_g[l],
                                     w_out[l], ln1_g[l], ln1_b[l], w_router[l], router_bias[l], w_exp_gate[l],
                                     w_exp_up[l], w_exp_down[l], w_sh_gate[l], w_sh_up[l], w_sh_down[l],
                                     ln2_g[l], ln2_b[l], prompt=False)
        kp.append(k1); vp.append(v1); sp.append(s1)
        kq.append(k2); vq.append(v2); sq.append(s2)
    new_k_prompt = jnp.stack(kp)
    new_v_prompt = jnp.stack(vp)
    new_s_prompt = jnp.stack(sp)
    new_k_sample = jnp.stack(kq)
    new_v_sample = jnp.stack(vq)
    new_s_sample = jnp.stack(sq)
    return (yp, ys, new_k_prompt, new_v_prompt, new_s_prompt, new_k_sample, new_v_sample, new_s_sample)
```

```python
import functools

import jax
import jax.numpy as jnp
from jax import lax
from jax.experimental import pallas as pl
from jax.experimental.pallas import tpu as pltpu

F32 = jnp.float32
BF16 = jnp.bfloat16

D_MODEL = 2048
PAST_LEN = 16384
HEAD_DIM = 64
N_HEADS = 16
N_KV_HEADS = 4
WINDOW = 128
ROPE_DIM = 16
ROPE_THETA = 500000.0
GLA_HEADS = 4
GLA_DV = 256
GLA_DK = 128
GLA_RANK = 16
GLA_TAU = 16.0
Q_COLS = N_HEADS * HEAD_DIM
KV_COLS = N_KV_HEADS * HEAD_DIM
GLA_QK_COLS = GLA_HEADS * GLA_DK
GLA_WIDTH = GLA_HEADS * GLA_DV
MAIN_COLS = Q_COLS + 2 * KV_COLS + 2 * GLA_QK_COLS + 2 * GLA_WIDTH
TOP_K = 8
N_GROUPS = 8
TOPK_GROUPS = 4
ROUTED_SCALE = 2.5
DEPTH = 1
DEEPNORM_ALPHA = (2 * DEPTH) ** 0.25
LN_EPS = 1e-5
RMS_EPS = 1e-6

LANES = 128
COL_BLOCK = 512
GLA_CHUNK = 64
GLA_SUB = 16
EXP_CLAMP = 80.0
MOE_BLK = 128
VMEM_LIMIT = 56 * 1024 * 1024


def _pick(n, cands):
    for c in cands:
        if n % c == 0:
            return c
    raise ValueError(f"no tile in {cands} divides {n}")


def _params(sem):
    return pltpu.CompilerParams(dimension_semantics=sem, vmem_limit_bytes=VMEM_LIMIT)


def _dot(a, b):
    return jnp.dot(a, b, preferred_element_type=F32)


def _dot_nt(a, b):
    return lax.dot_general(a, b, (((1,), (1,)), ((), ())), preferred_element_type=F32)


def _dot_tn(a, b):
    return lax.dot_general(a, b, (((0,), (0,)), ((), ())), preferred_element_type=F32)


def _split3(x):
    hi = x.astype(BF16)
    r1 = x - hi.astype(F32)
    mid = r1.astype(BF16)
    lo = (r1 - mid.astype(F32)).astype(BF16)
    return hi, mid, lo


def _tile_lanes(t, width):
    reps = width // t.shape[-1]
    return t if reps == 1 else jnp.concatenate([t] * reps, axis=-1)


def _rope(x, c, sa, sb):
    w = x.shape[-1]
    return (x * _tile_lanes(c, w) + pltpu.roll(x, 8, 1) * _tile_lanes(sa, w)
            + pltpu.roll(x, w - 8, 1) * _tile_lanes(sb, w))


def _rope_tables(pos):
    half = ROPE_DIM // 2
    inv = 1.0 / (ROPE_THETA ** (jnp.arange(0, ROPE_DIM, 2, dtype=F32) / ROPE_DIM))
    ang = pos.astype(F32)[:, None] * inv[None, :]
    cos, sin = jnp.cos(ang), jnp.sin(ang)
    n = pos.shape[0]
    rest = HEAD_DIM - ROPE_DIM
    c = jnp.concatenate([cos, cos, jnp.ones((n, rest), F32)], axis=1)
    sa = jnp.concatenate([jnp.zeros((n, half), F32), sin, jnp.zeros((n, rest), F32)], axis=1)
    sb = jnp.concatenate([-sin, jnp.zeros((n, half + rest), F32)], axis=1)
    rep = LANES // HEAD_DIM
    return jnp.tile(c, (1, rep)), jnp.tile(sa, (1, rep)), jnp.tile(sb, (1, rep))


def _inproj_kernel(x_ref, w_ref, wag_ref, wa2_ref, ba_ref, h_ref, la_ref, xb_ref):
    @pl.when(pl.program_id(1) == 0)
    def _():
        xb = x_ref[...].astype(BF16)
        xb_ref[...] = xb
        ag = _dot(xb, wag_ref[...].astype(BF16))
        z = _dot(ag.astype(BF16), wa2_ref[...].astype(BF16)) + ba_ref[...]
        la_ref[...] = (jnp.minimum(z, 0.0) - jnp.log1p(jnp.exp(-jnp.abs(z)))) * (1.0 / GLA_TAU)

    h_ref[...] = _dot(xb_ref[...], w_ref[...].astype(BF16))


def _inproj(x, w_in, wag, wa2, ba):
    t = x.shape[0]
    tm = _pick(t, (1024, 512, 256, 128))
    ncol = MAIN_COLS // COL_BLOCK
    return pl.pallas_call(
        _inproj_kernel,
        grid=(t // tm, ncol),
        in_specs=[
            pl.BlockSpec((tm, D_MODEL), lambda i, j: (i, 0)),
            pl.BlockSpec((D_MODEL, COL_BLOCK), lambda i, j: (0, j)),
            pl.BlockSpec((D_MODEL, LANES), lambda i, j: (0, 0)),
            pl.BlockSpec((LANES, GLA_QK_COLS), lambda i, j: (0, 0)),
            pl.BlockSpec((1, GLA_QK_COLS), lambda i, j: (0, 0)),
        ],
        out_specs=[
            pl.BlockSpec((tm, COL_BLOCK), lambda i, j: (i, j)),
            pl.BlockSpec((tm, GLA_QK_COLS), lambda i, j: (i, 0)),
        ],
        out_shape=[
            jax.ShapeDtypeStruct((t, MAIN_COLS), F32),
            jax.ShapeDtypeStruct((t, GLA_QK_COLS), F32),
        ],
        scratch_shapes=[pltpu.VMEM((tm, D_MODEL), BF16)],
        compiler_params=_params(("arbitrary", "arbitrary")),
        name="inproj",
    )(x, w_in, wag, wa2, ba)


def _half_variants(x):
    lane = lax.broadcasted_iota(jnp.int32, (x.shape[0], LANES), 1)
    lo = lane < HEAD_DIM
    out = []
    for g in range(N_KV_HEADS):
        chunk = x[:, (g // 2) * LANES:(g // 2 + 1) * LANES]
        swapped = pltpu.roll(chunk, HEAD_DIM, 1)
        u = g % 2
        for b in range(2):
            src = chunk if b == u else swapped
            keep = lo if b == 0 else jnp.logical_not(lo)
            out.append(jnp.where(keep, src, 0.0).astype(BF16))
    return out


def _swa_kernel(q_ref, kv_ref, c_ref, sa_ref, sb_ref, sink_ref, g_ref,
                oa_ref, nk_ref, nv_ref, kp_ref, vp_ref):
    i = pl.program_id(0)
    blk = q_ref.shape[0]
    c, sa, sb = c_ref[...], sa_ref[...], sb_ref[...]

    @pl.when(i == 0)
    def _():
        kp_ref[...] = jnp.zeros_like(kp_ref)
        vp_ref[...] = jnp.zeros_like(vp_ref)

    q = (_rope(q_ref[...], c, sa, sb) * (HEAD_DIM ** -0.5)).astype(BF16)
    kv = kv_ref[...]
    k = _rope(kv[:, :KV_COLS], c, sa, sb)
    v = kv[:, KV_COLS:]
    kc = _half_variants(k)
    vc = _half_variants(v)

    row = lax.broadcasted_iota(jnp.int32, (blk, 2 * blk), 0)
    col = lax.broadcasted_iota(jnp.int32, (blk, 2 * blk), 1)
    first_key = jnp.where(i > 0, 0, blk)
    mask = (col > row) & (col <= row + WINDOW) & (col >= first_key)

    pairs = []
    for p in range(N_HEADS // 2):
        qp = q[:, p * LANES:(p + 1) * LANES]
        acc = jnp.zeros((blk, LANES), F32)
        for b in range(2):
            h = 2 * p + b
            g = h // (N_HEADS // N_KV_HEADS)
            kcat = jnp.concatenate([kp_ref[2 * g + b], kc[2 * g + b]], axis=0)
            vcat = jnp.concatenate([vp_ref[2 * g + b], vc[2 * g + b]], axis=0)
            s = jnp.where(mask, _dot_nt(qp, kcat), -jnp.inf)
            sk = sink_ref[h]
            m = jnp.maximum(jnp.max(s, axis=1, keepdims=True), sk)
            e = jnp.exp(s - m)
            den = jnp.sum(e, axis=1, keepdims=True) + jnp.exp(sk - m)
            acc = acc + _dot((e / den).astype(BF16), vcat)
        pairs.append(acc)
    o = jnp.concatenate(pairs, axis=1)
    ms = jnp.mean(o * o, axis=1, keepdims=True)
    oa_ref[...] = (o * lax.rsqrt(ms + RMS_EPS) * g_ref[...]).astype(oa_ref.dtype)

    for n in range(2 * N_KV_HEADS):
        kp_ref[n] = kc[n]
        vp_ref[n] = vc[n]

    @pl.when(i == pl.num_programs(0) - 1)
    def _():
        nk_ref[...] = k
        nv_ref[...] = v


def _swa_prompt(h, c, sa, sb, sinks, g):
    t = h.shape[0]
    blk = WINDOW
    nb = t // blk
    return pl.pallas_call(
        _swa_kernel,
        grid=(nb,),
        in_specs=[
            pl.BlockSpec((blk, Q_COLS), lambda i: (i, 0)),
            pl.BlockSpec((blk, 2 * KV_COLS), lambda i: (i, Q_COLS // (2 * KV_COLS))),
            pl.BlockSpec((blk, LANES), lambda i: (i, 0)),
            pl.BlockSpec((blk, LANES), lambda i: (i, 0)),
            pl.BlockSpec((blk, LANES), lambda i: (i, 0)),
            pl.BlockSpec(memory_space=pltpu.SMEM),
            pl.BlockSpec((1, Q_COLS), lambda i: (0, 0)),
        ],
        out_specs=[
            pl.BlockSpec((blk, Q_COLS), lambda i: (i, 0)),
            pl.BlockSpec((blk, KV_COLS), lambda i: (0, 0)),
            pl.BlockSpec((blk, KV_COLS), lambda i: (0, 0)),
        ],
        out_shape=[
            jax.ShapeDtypeStruct((t, Q_COLS), BF16),
            jax.ShapeDtypeStruct((blk, KV_COLS), F32),
            jax.ShapeDtypeStruct((blk, KV_COLS), F32),
        ],
        scratch_shapes=[pltpu.VMEM((2 * N_KV_HEADS, blk, LANES), BF16),
                        pltpu.VMEM((2 * N_KV_HEADS, blk, LANES), BF16)],
        compiler_params=_params(("arbitrary",)),
        name="swa_prompt",
    )(h, h, c, sa, sb, sinks, g)


def _silu(x):
    return x * jax.nn.sigmoid(x)


def _gla_kernel(qg_ref, kg_ref, v1_ref, v2_ref, r1_ref, r2_ref, la_ref, gn_ref,
                og_ref, sfin_ref, s_ref):
    i = pl.program_id(0)
    rows = qg_ref.shape[0]
    ch = GLA_CHUNK
    nsub = ch // GLA_SUB

    @pl.when(i == 0)
    def _():
        s_ref[...] = jnp.zeros_like(s_ref)

    rr = lax.broadcasted_iota(jnp.int32, (ch, ch), 0)
    cc = lax.broadcasted_iota(jnp.int32, (ch, ch), 1)
    tri = jnp.where(rr >= cc, 1.0, 0.0).astype(BF16)
    pre = jnp.where(cc < (rr // GLA_SUB) * GLA_SUB, 1.0, 0.0).astype(BF16)
    mr = lax.broadcasted_iota(jnp.int32, (ch, nsub * ch), 0)
    mc = lax.broadcasted_iota(jnp.int32, (ch, nsub * ch), 1)
    amask = ((mc // ch) == (mr // GLA_SUB)) & ((mc % ch) <= mr)
    ones_t = jnp.ones((ch, LANES), BF16)

    def chunk(ci, carry):
        r0 = pl.multiple_of(ci * ch, ch)
        for hh in range(GLA_HEADS):
            dk = slice(hh * GLA_DK, (hh + 1) * GLA_DK)
            vref, rref = (v1_ref, r1_ref) if hh < 2 else (v2_ref, r2_ref)
            dv = slice((hh % 2) * GLA_DV, (hh % 2 + 1) * GLA_DV)
            q = qg_ref[pl.ds(r0, ch), dk] * (GLA_DK ** -0.5)
            k = kg_ref[pl.ds(r0, ch), dk]
            la = la_ref[pl.ds(r0, ch), dk]
            v = vref[pl.ds(r0, ch), dv]
            r = rref[pl.ds(r0, ch), dv]
            hi, mid, lo = _split3(la)
            b = _dot(tri, hi) + _dot(tri, mid) + _dot(tri, lo)
            bs = _dot(pre, hi) + _dot(pre, mid) + _dot(pre, lo)
            blast = b[ch - 1:ch, :]
            bl_col = _dot_tn(hi, ones_t) + _dot_tn(mid, ones_t) + _dot_tn(lo, ones_t)
            dec_col = jnp.exp(jnp.concatenate([bl_col, bl_col], axis=1))

            qt = (q * jnp.exp(b - bs)).astype(BF16)
            kparts = []
            for si in range(nsub):
                ref_row = bs[si * GLA_SUB:si * GLA_SUB + 1, :]
                kparts.append((k * jnp.exp(jnp.minimum(ref_row - b, EXP_CLAMP))).astype(BF16))
            kcat = jnp.concatenate(kparts, axis=0)
            a = jnp.where(amask, _dot_nt(qt, kcat), 0.0).astype(BF16)
            vb = v.astype(BF16)
            vrep = jnp.concatenate([vb] * nsub, axis=0)
            s_old = s_ref[hh]
            o = _dot(a, vrep) + _dot((q * jnp.exp(b)).astype(BF16), s_old.astype(BF16))
            kd = (k * jnp.exp(blast - b)).astype(BF16)
            s_ref[hh] = dec_col * s_old + _dot_tn(kd, vb)

            ms = jnp.mean(o * o, axis=1, keepdims=True)
            gn = gn_ref[:, hh * GLA_DV:(hh + 1) * GLA_DV]
            y = o * lax.rsqrt(ms + RMS_EPS) * gn
            og_ref[pl.ds(r0, ch), hh * GLA_DV:(hh + 1) * GLA_DV] = (y * _silu(r)).astype(og_ref.dtype)
        return carry

    lax.fori_loop(0, rows // ch, chunk, 0)

    @pl.when(i == pl.num_programs(0) - 1)
    def _():
        sfin_ref[...] = s_ref[...]


def _gla_prompt(h, la, gn):
    t = h.shape[0]
    rows = _pick(t, (512, 256, 128, 64))
    cb = COL_BLOCK
    base = (Q_COLS + 2 * KV_COLS) // cb
    spec = lambda j: pl.BlockSpec((rows, cb), lambda i: (i, j))
    return pl.pallas_call(
        _gla_kernel,
        grid=(t // rows,),
        in_specs=[spec(base), spec(base + 1), spec(base + 2), spec(base + 3), spec(base + 4), spec(base + 5),
                  pl.BlockSpec((rows, GLA_QK_COLS), lambda i: (i, 0)),
                  pl.BlockSpec((1, GLA_WIDTH), lambda i: (0, 0))],
        out_specs=[
            pl.BlockSpec((rows, GLA_WIDTH), lambda i: (i, 0)),
            pl.BlockSpec((GLA_HEADS, GLA_DK, GLA_DV), lambda i: (0, 0, 0)),
        ],
        out_shape=[
            jax.ShapeDtypeStruct((t, GLA_WIDTH), BF16),
            jax.ShapeDtypeStruct((GLA_HEADS, GLA_DK, GLA_DV), F32),
        ],
        scratch_shapes=[pltpu.VMEM((GLA_HEADS, GLA_DK, GLA_DV), F32)],
        compiler_params=_params(("arbitrary",)),
        name="gla_prompt",
    )(h, h, h, h, h, h, la, gn)


def _dup_heads(x):
    lane = lax.broadcasted_iota(jnp.int32, (x.shape[0], LANES), 1)
    lo = lane < HEAD_DIM
    out = []
    for g in range(N_KV_HEADS):
        chunk = x[:, (g // 2) * LANES:(g // 2 + 1) * LANES]
        swapped = pltpu.roll(chunk, HEAD_DIM, 1)
        out.append(jnp.where(lo, chunk, swapped) if g % 2 == 0 else jnp.where(lo, swapped, chunk))
    return out


def _swa_dec_kernel(q_ref, kv_ref, ck_ref, cv_ref, c_ref, sa_ref, sb_ref, sink_ref, g_ref, e_ref, et_ref,
                    oa_ref, nk_ref, nv_ref):
    win = ck_ref.shape[1]
    c, sa, sb = c_ref[...], sa_ref[...], sb_ref[...]
    q8 = jnp.broadcast_to(q_ref[0], (8, Q_COLS))
    kv8 = jnp.broadcast_to(kv_ref[0], (8, 2 * KV_COLS))
    q = _rope(q8, c, sa, sb) * (HEAD_DIM ** -0.5)
    knew = _rope(kv8[:, :KV_COLS], c, sa, sb)
    vnew = kv8[:, KV_COLS:]

    row = lax.broadcasted_iota(jnp.int32, (win, KV_COLS), 0)
    last = row == win - 1
    keys = jnp.where(last, jnp.broadcast_to(knew[0:1], (win, KV_COLS)), pltpu.roll(ck_ref[0], win - 1, 0))
    vals = jnp.where(last, jnp.broadcast_to(vnew[0:1], (win, KV_COLS)), pltpu.roll(cv_ref[0], win - 1, 0))
    nk_ref[0] = keys
    nv_ref[0] = vals

    qb = q.astype(BF16).astype(F32)
    kd = _dup_heads(keys.astype(BF16).astype(F32))
    vd = _dup_heads(vals.astype(BF16).astype(F32))
    npair = N_HEADS // 2
    prod = jnp.concatenate(
        [kd[p // 2] * jnp.broadcast_to(qb[0:1, p * LANES:(p + 1) * LANES], (win, LANES)) for p in range(npair)],
        axis=1)
    st = _dot(prod.astype(BF16), e_ref[...])
    sk = sink_ref[...]
    m = jnp.maximum(jnp.max(st, axis=0, keepdims=True), sk)
    e = jnp.exp(st - m)
    den = jnp.sum(e, axis=0, keepdims=True) + jnp.exp(sk - m)
    pe = _dot((e / den).astype(BF16), et_ref[...])
    vcat = jnp.concatenate([vd[p // 2] for p in range(npair)], axis=1)
    o = jnp.sum(pe * vcat, axis=0, keepdims=True)
    ms = jnp.mean(o * o, axis=1, keepdims=True)
    oa_ref[0] = (o * lax.rsqrt(ms + RMS_EPS) * g_ref[...]).astype(oa_ref.dtype)


def _swa_decode(h3, ck, cv, c, sa, sb, sinks, g, e, et):
    bsz = h3.shape[0]
    win = ck.shape[1]
    const = lambda shape: pl.BlockSpec(shape, lambda b: tuple(0 for _ in shape))
    return pl.pallas_call(
        _swa_dec_kernel,
        grid=(bsz,),
        in_specs=[
            pl.BlockSpec((1, 1, Q_COLS), lambda b: (b, 0, 0)),
            pl.BlockSpec((1, 1, 2 * KV_COLS), lambda b: (b, 0, Q_COLS // (2 * KV_COLS))),
            pl.BlockSpec((1, win, KV_COLS), lambda b: (b, 0, 0)),
            pl.BlockSpec((1, win, KV_COLS), lambda b: (b, 0, 0)),
            const((1, LANES)), const((1, LANES)), const((1, LANES)), const((1, LANES)),
            const((1, Q_COLS)), const((Q_COLS, LANES)), const((LANES, Q_COLS)),
        ],
        out_specs=[
            pl.BlockSpec((1, 1, Q_COLS), lambda b: (b, 0, 0)),
            pl.BlockSpec((1, win, KV_COLS), lambda b: (b, 0, 0)),
            pl.BlockSpec((1, win, KV_COLS), lambda b: (b, 0, 0)),
        ],
        out_shape=[
            jax.ShapeDtypeStruct((bsz, 1, Q_COLS), BF16),
            jax.ShapeDtypeStruct((bsz, win, KV_COLS), F32),
            jax.ShapeDtypeStruct((bsz, win, KV_COLS), F32),
        ],
        compiler_params=_params(("arbitrary",)),
        name="swa_decode",
    )(h3, h3, ck, cv, c, sa, sb, sinks, g, e, et)


def _rows8(rows):
    w = rows[0].shape[1]
    ridx = lax.broadcasted_iota(jnp.int32, (8, w), 0)
    out = jnp.zeros((8, w), F32)
    for n, r in enumerate(rows):
        out = jnp.where(ridx == n, jnp.broadcast_to(r, (8, w)), out)
    return out


def _gla_dec_kernel(qg_ref, kg_ref, v1_ref, v2_ref, r1_ref, r2_ref, la_ref, s_ref, gn_ref, og_ref, sn_ref):
    ones8 = jnp.where(lax.broadcasted_iota(jnp.int32, (8, GLA_DV), 0) < 3, 1.0, 0.0).astype(BF16)
    outs = []
    for hh in range(GLA_HEADS):
        dk = slice(hh * GLA_DK, (hh + 1) * GLA_DK)
        vref, rref = (v1_ref, r1_ref) if hh < 2 else (v2_ref, r2_ref)
        dv = slice((hh % 2) * GLA_DV, (hh % 2 + 1) * GLA_DV)
        q = qg_ref[0][:, dk] * (GLA_DK ** -0.5)
        k = kg_ref[0][:, dk]
        la = la_ref[0][:, dk]
        v = vref[0][:, dv]
        r = rref[0][:, dv]
        s_old = s_ref[0, hh]
        dec = jnp.exp(la)
        qb = q.astype(BF16).astype(F32)
        kb = k.astype(BF16).astype(F32)
        score = jnp.sum(qb * kb, axis=1, keepdims=True)
        q8 = _rows8([q * dec]).astype(BF16)
        o = score * v + _dot(q8, s_old.astype(BF16))[0:1, :]
        dhi, dmid, dlo = _split3(dec)
        d8 = _rows8([dhi.astype(F32), dmid.astype(F32), dlo.astype(F32)]).astype(BF16)
        dec_col = _dot_tn(d8, ones8)
        k8 = _rows8([k]).astype(BF16)
        v8 = _rows8([v]).astype(BF16)
        sn_ref[0, hh] = dec_col * s_old + _dot_tn(k8, v8)
        ms = jnp.mean(o * o, axis=1, keepdims=True)
        gn = gn_ref[:, hh * GLA_DV:(hh + 1) * GLA_DV]
        outs.append(o * lax.rsqrt(ms + RMS_EPS) * gn * _silu(r))
    og_ref[0] = jnp.concatenate(outs, axis=1).astype(og_ref.dtype)


def _gla_decode(h3, la3, s0, gn):
    bsz = h3.shape[0]
    cb = COL_BLOCK
    base = (Q_COLS + 2 * KV_COLS) // cb
    spec = lambda j: pl.BlockSpec((1, 1, cb), lambda b: (b, 0, j))
    return pl.pallas_call(
        _gla_dec_kernel,
        grid=(bsz,),
        in_specs=[spec(base), spec(base + 1), spec(base + 2), spec(base + 3), spec(base + 4), spec(base + 5),
                  pl.BlockSpec((1, 1, GLA_QK_COLS), lambda b: (b, 0, 0)),
                  pl.BlockSpec((1, GLA_HEADS, GLA_DK, GLA_DV), lambda b: (b, 0, 0, 0)),
                  pl.BlockSpec((1, GLA_WIDTH), lambda b: (0, 0))],
        out_specs=[
            pl.BlockSpec((1, 1, GLA_WIDTH), lambda b: (b, 0, 0)),
            pl.BlockSpec((1, GLA_HEADS, GLA_DK, GLA_DV), lambda b: (b, 0, 0, 0)),
        ],
        out_shape=[
            jax.ShapeDtypeStruct((bsz, 1, GLA_WIDTH), BF16),
            jax.ShapeDtypeStruct(s0.shape, s0.dtype),
        ],
        compiler_params=_params(("arbitrary",)),
        name="gla_decode",
    )(h3, h3, h3, h3, h3, h3, la3, s0, gn)


def _layernorm(v, g, b):
    mu = jnp.mean(v, axis=1, keepdims=True)
    d = v - mu
    var = jnp.mean(d * d, axis=1, keepdims=True)
    return d * lax.rsqrt(var + LN_EPS) * g + b


def _load_bf16(w_hbm, dst_ref, stage_ref, sem, chunk):
    for s in range(w_hbm.shape[0] // chunk):
        cp = pltpu.make_async_copy(w_hbm.at[pl.ds(s * chunk, chunk), :], stage_ref, sem)
        cp.start()
        cp.wait()
        dst_ref[pl.ds(s * chunk, chunk), :] = stage_ref[...].astype(BF16)


def _route(logits, bias):
    rows, ne = logits.shape
    per = ne // N_GROUPS
    ninf = -jnp.inf
    lane = lax.broadcasted_iota(jnp.int32, (rows, ne), 1)
    gid = lane // per
    scores = jax.nn.sigmoid(logits)
    biased = scores + bias
    gfull = jnp.zeros((rows, ne), F32)
    gcols = []
    for g in range(N_GROUPS):
        ing = gid == g
        xg = jnp.where(ing, biased, ninf)
        m1 = jnp.max(xg, axis=1, keepdims=True)
        cnt = jnp.sum(jnp.where(xg == m1, 1.0, 0.0), axis=1, keepdims=True)
        m2 = jnp.max(jnp.where(xg < m1, xg, ninf), axis=1, keepdims=True)
        gs = m1 + jnp.where(cnt >= 2.0, m1, m2)
        gcols.append(gs)
        gfull = jnp.where(ing, gs, gfull)
    rank = jnp.zeros((rows, ne), jnp.int32)
    for g in range(N_GROUPS):
        beats = (gcols[g] > gfull) | ((gcols[g] == gfull) & (gid > g))
        rank = rank + jnp.where(beats, 1, 0)
    masked = jnp.where(rank < TOPK_GROUPS, biased, ninf)

    lane_f = lane.astype(F32)
    out_lane = lax.broadcasted_iota(jnp.int32, (rows, LANES), 1)
    eout = jnp.zeros((rows, LANES), F32)
    gout = jnp.zeros((rows, LANES), F32)
    total = jnp.zeros((rows, 1), F32)
    for kk in range(TOP_K):
        m = jnp.max(masked, axis=1, keepdims=True)
        idx = jnp.min(jnp.where(masked == m, lane_f, float(ne)), axis=1, keepdims=True)
        hit = lane_f == idx
        gk = jnp.sum(jnp.where(hit, scores, 0.0), axis=1, keepdims=True)
        masked = jnp.where(hit, ninf, masked)
        total = total + gk
        eout = jnp.where(out_lane == kk, idx, eout)
        gout = jnp.where(out_lane == kk, gk, gout)
    return eout.astype(jnp.int32), gout / total * ROUTED_SCALE


def _outproj_kernel(x_ref, oa_ref, og_ref, wout_hbm, g1_ref, b1_ref, wr_ref, rb_ref,
                    x1_ref, eidx_ref, gate_ref, wo_ref, stage_ref, sem):
    @pl.when(pl.program_id(0) == 0)
    def _():
        _load_bf16(wout_hbm, wo_ref, stage_ref, sem, stage_ref.shape[0])

    half = oa_ref.shape[1]
    mix = _dot(oa_ref[...], wo_ref[pl.ds(0, half), :]) + _dot(og_ref[...], wo_ref[pl.ds(half, half), :])
    x1 = _layernorm(DEEPNORM_ALPHA * x_ref[...] + mix, g1_ref[...], b1_ref[...])
    x1_ref[...] = x1
    logits = _dot(x1.astype(BF16), wr_ref[...].astype(BF16))
    eidx, gates = _route(logits, rb_ref[...])
    eidx_ref[...] = eidx
    gate_ref[...] = gates


def _outproj_router(x, oa, og, w_out, g1, b1, w_router, rbias):
    t = x.shape[0]
    tm = _pick(t, (384, 256, 128, 64, 8))
    ne = w_router.shape[1]
    const = lambda shape: pl.BlockSpec(shape, lambda i: tuple(0 for _ in shape))
    return pl.pallas_call(
        _outproj_kernel,
        grid=(t // tm,),
        in_specs=[
            pl.BlockSpec((tm, D_MODEL), lambda i: (i, 0)),
            pl.BlockSpec((tm, Q_COLS), lambda i: (i, 0)),
            pl.BlockSpec((tm, GLA_WIDTH), lambda i: (i, 0)),
            pl.BlockSpec(memory_space=pl.ANY),
            const((1, D_MODEL)), const((1, D_MODEL)), const((D_MODEL, ne)), const((1, ne)),
        ],
        out_specs=[
            pl.BlockSpec((tm, D_MODEL), lambda i: (i, 0)),
            pl.BlockSpec((tm, LANES), lambda i: (i, 0)),
            pl.BlockSpec((tm, LANES), lambda i: (i, 0)),
        ],
        out_shape=[
            jax.ShapeDtypeStruct((t, D_MODEL), F32),
            jax.ShapeDtypeStruct((t, LANES), jnp.int32),
            jax.ShapeDtypeStruct((t, LANES), F32),
        ],
        scratch_shapes=[pltpu.VMEM((D_MODEL, D_MODEL), BF16),
                        pltpu.VMEM((512, D_MODEL), F32),
                        pltpu.SemaphoreType.DMA(())],
        compiler_params=_params(("arbitrary",)),
        name="outproj_router",
    )(x, oa, og, w_out, g1, b1, w_router, rbias)


def _moe_kernel(be_ref, nu_ref, src_hbm, dst_hbm, x_hbm, wg_ref, wu_ref, wd_ref, y_hbm,
                xbuf, ybuf, wgb, wub, wdb, srcs, dsts, gsem, ssem, isem):
    i = pl.program_id(0)
    nused = nu_ref[0]
    blk = xbuf.shape[1]

    def idx_copies(j):
        s = j % 3
        return (pltpu.make_async_copy(src_hbm.at[j], srcs.at[s], isem.at[0, s]),
                pltpu.make_async_copy(dst_hbm.at[j], dsts.at[s], isem.at[1, s]))

    def start_idx(j):
        for cp in idx_copies(j):
            cp.start()

    def wait_idx(j):
        for cp in idx_copies(j):
            cp.wait()

    def gather_copy(tok, r, slot):
        return pltpu.make_async_copy(x_hbm.at[pl.ds(tok, 1), :], xbuf.at[slot, pl.ds(r, 1), :], gsem.at[slot])

    def scatter_copy(dst, r, slot):
        return pltpu.make_async_copy(ybuf.at[slot, pl.ds(r, 1), :], y_hbm.at[pl.ds(dst, 1), :], ssem.at[slot])

    def issue_gather(j):
        s3, slot = j % 3, j % 2

        def body(r, carry):
            gather_copy(srcs[s3, r], r, slot).start()
            return carry
        lax.fori_loop(0, blk, body, 0)

    def wait_gather(slot):
        for r in range(blk):
            gather_copy(0, r, slot).wait()

    def issue_scatter(j):
        s3, slot = j % 3, j % 2

        def body(r, carry):
            scatter_copy(dsts[s3, r], r, slot).start()
            return carry
        lax.fori_loop(0, blk, body, 0)

    def wait_scatter(slot):
        for r in range(blk):
            scatter_copy(0, r, slot).wait()

    @pl.when(i < nused)
    def _():
        slot = i % 2

        @pl.when(i == 0)
        def _():
            ybuf[0] = jnp.zeros(ybuf.shape[1:], ybuf.dtype)
            tail = y_hbm.shape[0] - 2 * blk
            for part in range(2):
                cp = pltpu.make_async_copy(ybuf.at[0], y_hbm.at[pl.ds(tail + part * blk, blk), :], ssem.at[0])
                cp.start()
                cp.wait()
            start_idx(0)

            @pl.when(nused > 1)
            def _():
                start_idx(1)
            wait_idx(0)
            issue_gather(0)

        @pl.when(i + 2 < nused)
        def _():
            start_idx(i + 2)

        @pl.when(i + 1 < nused)
        def _():
            wait_idx(i + 1)
            issue_gather(i + 1)

        prev = be_ref[jnp.maximum(i - 1, 0)]

        @pl.when((i == 0) | (be_ref[i] != prev))
        def _():
            wgb[...] = wg_ref[0].astype(BF16)
            wub[...] = wu_ref[0].astype(BF16)
            wdb[...] = wd_ref[0].astype(BF16)

        wait_gather(slot)

        @pl.when(i >= 2)
        def _():
            wait_scatter(slot)

        xb = xbuf[slot].astype(BF16)
        hact = _silu(_dot(xb, wgb[...])) * _dot(xb, wub[...])
        ybuf[slot] = _dot(hact.astype(BF16), wdb[...])
        issue_scatter(i)

        @pl.when(i == nused - 1)
        def _():
            @pl.when(i >= 1)
            def _():
                wait_scatter(1 - slot)
            wait_scatter(slot)


def _moe_experts(x1, src, dst, block_expert, nused, w_gate, w_up, w_down, n_out_rows):
    nb = src.shape[0]
    ne, d, ff = w_gate.shape
    grid_spec = pltpu.PrefetchScalarGridSpec(
        num_scalar_prefetch=2,
        grid=(nb,),
        in_specs=[
            pl.BlockSpec(memory_space=pl.ANY),
            pl.BlockSpec(memory_space=pl.ANY),
            pl.BlockSpec(memory_space=pl.ANY),
            pl.BlockSpec((1, d, ff), lambda i, be, nu: (be[i], 0, 0)),
            pl.BlockSpec((1, d, ff), lambda i, be, nu: (be[i], 0, 0)),
            pl.BlockSpec((1, ff, d), lambda i, be, nu: (be[i], 0, 0)),
        ],
        out_specs=pl.BlockSpec(memory_space=pl.ANY),
        scratch_shapes=[
            pltpu.VMEM((2, MOE_BLK, d), F32),
            pltpu.VMEM((2, MOE_BLK, d), F32),
            pltpu.VMEM((d, ff), BF16),
            pltpu.VMEM((d, ff), BF16),
            pltpu.VMEM((ff, d), BF16),
            pltpu.SMEM((3, MOE_BLK), jnp.int32),
            pltpu.SMEM((3, MOE_BLK), jnp.int32),
            pltpu.SemaphoreType.DMA((2,)),
            pltpu.SemaphoreType.DMA((2,)),
            pltpu.SemaphoreType.DMA((2, 3)),
        ],
    )
    return pl.pallas_call(
        _moe_kernel,
        grid_spec=grid_spec,
        out_shape=jax.ShapeDtypeStruct((n_out_rows, d), F32),
        compiler_params=_params(("arbitrary",)),
        name="moe_experts",
    )(block_expert, nused, src, dst, x1, w_gate, w_up, w_down)


def _dispatch_plan(eidx, ne):
    t = eidx.shape[0]
    n = t * TOP_K
    blk = MOE_BLK
    nb = -(-(n + ne * (blk - 1)) // blk)
    flat_e = eidx.reshape(-1)
    counts = jnp.bincount(flat_e, length=ne).astype(jnp.int32)
    starts = jnp.cumsum(counts) - counts
    padded = (counts + blk - 1) // blk * blk
    pend = jnp.cumsum(padded)
    pstart = pend - padded
    order = jnp.argsort(flat_e).astype(jnp.int32)
    se = flat_e[order]
    slot = pstart[se] + jnp.arange(n, dtype=jnp.int32) - starts[se]
    tok = order // TOP_K
    kk = order % TOP_K
    pos = jnp.arange(nb * blk, dtype=jnp.int32)
    dump = TOP_K * t + ((pos // blk) % 2) * blk + pos % blk
    src = jnp.zeros((nb * blk,), jnp.int32).at[slot].set(tok)
    dst = dump.at[slot].set(kk * t + tok)
    nused = (pend[-1] // blk).astype(jnp.int32)
    bidx = jnp.arange(nb, dtype=jnp.int32)
    bexp = jnp.minimum(jnp.searchsorted(pend, bidx * blk, side='right'), ne - 1).astype(jnp.int32)
    bexp = jnp.where(bidx < nused, bexp, bexp[jnp.maximum(nused - 1, 0)])
    return src.reshape(nb, blk), dst.reshape(nb, blk), bexp, nused.reshape(1)


def _final_kernel(*refs):
    x1_ref, gate_ref = refs[0], refs[1]
    y_refs = refs[2:2 + TOP_K]
    wsg_hbm, wsu_hbm, wsd_hbm, g2_ref, b2_ref, out_ref, wsg, wsu, wsd, stage_a, stage_b, sem = refs[2 + TOP_K:]

    @pl.when(pl.program_id(0) == 0)
    def _():
        _load_bf16(wsg_hbm, wsg, stage_a, sem, stage_a.shape[0])
        _load_bf16(wsu_hbm, wsu, stage_a, sem, stage_a.shape[0])
        _load_bf16(wsd_hbm, wsd, stage_b, sem, stage_b.shape[0])

    x1 = x1_ref[...]
    xb = x1.astype(BF16)
    hact = _silu(_dot(xb, wsg[...])) * _dot(xb, wsu[...])
    f = _dot(hact.astype(BF16), wsd[...])
    gates = gate_ref[...]
    routed = jnp.zeros_like(x1)
    for kk in range(TOP_K):
        routed = routed + y_refs[kk][...] * gates[:, kk:kk + 1]
    out_ref[...] = _layernorm(DEEPNORM_ALPHA * x1 + (f + routed), g2_ref[...], b2_ref[...])


def _final(x1, gates, ycomb, w_sg, w_su, w_sd, g2, b2):
    t = x1.shape[0]
    tm = _pick(t, (128, 64, 8))
    nt = t // tm
    ff = w_sg.shape[1]
    const = lambda shape: pl.BlockSpec(shape, lambda i: tuple(0 for _ in shape))
    yspec = lambda kk: pl.BlockSpec((tm, D_MODEL), lambda i: (kk * nt + i, 0))
    anyspec = pl.BlockSpec(memory_space=pl.ANY)
    return pl.pallas_call(
        _final_kernel,
        grid=(nt,),
        in_specs=[pl.BlockSpec((tm, D_MODEL), lambda i: (i, 0)),
                  pl.BlockSpec((tm, LANES), lambda i: (i, 0))]
                 + [yspec(kk) for kk in range(TOP_K)]
                 + [anyspec, anyspec, anyspec, const((1, D_MODEL)), const((1, D_MODEL))],
        out_specs=pl.BlockSpec((tm, D_MODEL), lambda i: (i, 0)),
        out_shape=jax.ShapeDtypeStruct((t, D_MODEL), F32),
        scratch_shapes=[pltpu.VMEM((D_MODEL, ff), BF16),
                        pltpu.VMEM((D_MODEL, ff), BF16),
                        pltpu.VMEM((ff, D_MODEL), BF16),
                        pltpu.VMEM((512, ff), F32),
                        pltpu.VMEM((128, D_MODEL), F32),
                        pltpu.SemaphoreType.DMA(())],
        compiler_params=_params(("arbitrary",)),
        name="shared_combine_ln",
    )(x1, gates, *([ycomb] * TOP_K), w_sg, w_su, w_sd, g2, b2)


def kernel(x_prompt, x_sample, cache_k_win, cache_v_win, state_gla, w_in, attn_sinks, attn_norm_g, w_gla_a2,
           b_gla_a, gla_norm_g, w_out, ln1_g, ln1_b, w_router, router_bias, w_exp_gate, w_exp_up, w_exp_down,
           w_sh_gate, w_sh_up, w_sh_down, ln2_g, ln2_b):
    assert w_in.shape[0] == DEPTH == 1 and x_prompt.shape[0] == 1 and x_sample.shape[1] == 1
    t = x_prompt.shape[1]
    bsz = x_sample.shape[0]
    win = cache_k_win.shape[2]
    assert t % WINDOW == 0 and win == WINDOW

    w_in0 = w_in[0]
    wag = jnp.pad(w_in0[:, MAIN_COLS:], ((0, 0), (0, LANES - GLA_RANK)))
    wa2 = jnp.pad(w_gla_a2[0], ((0, LANES - GLA_RANK), (0, 0)))
    ba = b_gla_a[0].reshape(1, GLA_QK_COLS)
    sinks = jnp.pad(attn_sinks[0].reshape(1, N_HEADS), ((0, 0), (0, LANES - N_HEADS)))
    ag = attn_norm_g[0].reshape(1, Q_COLS)
    gn = gla_norm_g[0].reshape(1, GLA_WIDTH)

    xp = x_prompt[0]
    hp, lap = _inproj(xp, w_in0, wag, wa2, ba)
    cp, sap, sbp = _rope_tables(jnp.arange(t, dtype=jnp.int32))
    oa_p, nk_p, nv_p = _swa_prompt(hp, cp, sap, sbp, attn_sinks[0], ag)
    og_p, s_p = _gla_prompt(hp, lap, gn)

    xs = x_sample[:, 0]
    hs, las = _inproj(xs, w_in0, wag, wa2, ba)
    cs, sas, sbs = _rope_tables(PAST_LEN + jnp.arange(1, dtype=jnp.int32))
    lane_head = jnp.arange(Q_COLS, dtype=jnp.int32)[:, None] // HEAD_DIM
    emat = (lane_head == jnp.arange(LANES, dtype=jnp.int32)[None, :]).astype(BF16)
    hs3 = hs.reshape(bsz, 1, MAIN_COLS)
    oa_s, nk_s, nv_s = _swa_decode(hs3, cache_k_win[0].reshape(bsz, win, KV_COLS),
                                   cache_v_win[0].reshape(bsz, win, KV_COLS), cs, sas, sbs, sinks, ag,
                                   emat, emat.T)
    og_s, s_s = _gla_decode(hs3, las.reshape(bsz, 1, GLA_QK_COLS), state_gla[0], gn)

    x_all = jnp.concatenate([xp, xs], axis=0)
    oa_all = jnp.concatenate([oa_p, oa_s.reshape(bsz, Q_COLS)], axis=0)
    og_all = jnp.concatenate([og_p, og_s.reshape(bsz, GLA_WIDTH)], axis=0)
    t_all = t + bsz
    x1, eidx, gates = _outproj_router(x_all, oa_all, og_all, w_out[0], ln1_g[0].reshape(1, D_MODEL),
                                      ln1_b[0].reshape(1, D_MODEL), w_router[0],
                                      router_bias[0].reshape(1, -1))
    ne = w_router.shape[2]
    src, dst, bexp, nused = _dispatch_plan(eidx[:, :TOP_K], ne)
    ycomb = _moe_experts(x1, src, dst, bexp, nused, w_exp_gate[0], w_exp_up[0], w_exp_down[0],
                         TOP_K * t_all + 2 * MOE_BLK)
    y_all = _final(x1, gates, ycomb, w_sh_gate[0], w_sh_up[0], w_sh_down[0],
                   ln2_g[0].reshape(1, D_MODEL), ln2_b[0].reshape(1, D_MODEL))

    y_prompt = y_all[:t].reshape(1, t, D_MODEL)
    y_sample = y_all[t:].reshape(bsz, 1, D_MODEL)
    new_k_prompt = nk_p.reshape(1, 1, win, N_KV_HEADS, HEAD_DIM)
    new_v_prompt = nv_p.reshape(1, 1, win, N_KV_HEADS, HEAD_DIM)
    new_s_prompt = s_p.reshape(1, 1, GLA_HEADS, GLA_DK, GLA_DV)
    new_k_sample = nk_s.reshape(1, bsz, win, N_KV_HEADS, HEAD_DIM)
    new_v_sample = nv_s.reshape(1, bsz, win, N_KV_HEADS, HEAD_DIM)
    new_s_sample = s_s.reshape(1, bsz, GLA_HEADS, GLA_DK, GLA_DV)
    return (y_prompt, y_sample, new_k_prompt, new_v_prompt, new_s_prompt,
            new_k_sample, new_v_sample, new_s_sample)
```

```python
import functools

import jax
import jax.numpy as jnp
from jax import lax
from jax.experimental import pallas as pl
from jax.experimental.pallas import tpu as pltpu

F32 = jnp.float32
BF16 = jnp.bfloat16

D_MODEL = 2048
PAST_LEN = 16384
HEAD_DIM = 64
N_HEADS = 16
N_KV_HEADS = 4
WINDOW = 128
ROPE_DIM = 16
ROPE_THETA = 500000.0
GLA_HEADS = 4
GLA_DV = 256
GLA_DK = 128
GLA_RANK = 16
GLA_TAU = 16.0
Q_COLS = N_HEADS * HEAD_DIM
KV_COLS = N_KV_HEADS * HEAD_DIM
GLA_QK_COLS = GLA_HEADS * GLA_DK
GLA_WIDTH = GLA_HEADS * GLA_DV
MAIN_COLS = Q_COLS + 2 * KV_COLS + 2 * GLA_QK_COLS + 2 * GLA_WIDTH
TOP_K = 8
N_GROUPS = 8
TOPK_GROUPS = 4
ROUTED_SCALE = 2.5
DEPTH = 1
DEEPNORM_ALPHA = (2 * DEPTH) ** 0.25
LN_EPS = 1e-5
RMS_EPS = 1e-6

LANES = 128
COL_BLOCK = 512
GLA_CHUNK = 64
GLA_SUB = 16
EXP_CLAMP = 80.0
MOE_BLK = 128
VMEM_LIMIT = 56 * 1024 * 1024


def _pick(n, cands):
    for c in cands:
        if n % c == 0:
            return c
    raise ValueError(f"no tile in {cands} divides {n}")


def _params(sem):
    return pltpu.CompilerParams(dimension_semantics=sem, vmem_limit_bytes=VMEM_LIMIT)


def _dot(a, b):
    return jnp.dot(a, b, preferred_element_type=F32)


def _dot_nt(a, b):
    return lax.dot_general(a, b, (((1,), (1,)), ((), ())), preferred_element_type=F32)


def _dot_tn(a, b):
    return lax.dot_general(a, b, (((0,), (0,)), ((), ())), preferred_element_type=F32)


def _split3(x):
    hi = x.astype(BF16)
    r1 = x - hi.astype(F32)
    mid = r1.astype(BF16)
    lo = (r1 - mid.astype(F32)).astype(BF16)
    return hi, mid, lo


def _tile_lanes(t, width):
    reps = width // t.shape[-1]
    return t if reps == 1 else jnp.concatenate([t] * reps, axis=-1)


def _rope(x, c, sa, sb):
    w = x.shape[-1]
    return (x * _tile_lanes(c, w) + pltpu.roll(x, 8, 1) * _tile_lanes(sa, w)
            + pltpu.roll(x, w - 8, 1) * _tile_lanes(sb, w))


def _rope_tables(pos):
    half = ROPE_DIM // 2
    inv = 1.0 / (ROPE_THETA ** (jnp.arange(0, ROPE_DIM, 2, dtype=F32) / ROPE_DIM))
    ang = pos.astype(F32)[:, None] * inv[None, :]
    cos, sin = jnp.cos(ang), jnp.sin(ang)
    n = pos.shape[0]
    rest = HEAD_DIM - ROPE_DIM
    c = jnp.concatenate([cos, cos, jnp.ones((n, rest), F32)], axis=1)
    sa = jnp.concatenate([jnp.zeros((n, half), F32), sin, jnp.zeros((n, rest), F32)], axis=1)
    sb = jnp.concatenate([-sin, jnp.zeros((n, half + rest), F32)], axis=1)
    rep = LANES // HEAD_DIM
    return jnp.tile(c, (1, rep)), jnp.tile(sa, (1, rep)), jnp.tile(sb, (1, rep))


def _inproj_kernel(x_ref, w_ref, wag_ref, wa2_ref, ba_ref, h_ref, la_ref, xb_ref):
    @pl.when(pl.program_id(1) == 0)
    def _():
        xb = x_ref[...].astype(BF16)
        xb_ref[...] = xb
        ag = _dot(xb, wag_ref[...].astype(BF16))
        z = _dot(ag.astype(BF16), wa2_ref[...].astype(BF16)) + ba_ref[...]
        la_ref[...] = (jnp.minimum(z, 0.0) - jnp.log1p(jnp.exp(-jnp.abs(z)))) * (1.0 / GLA_TAU)

    h_ref[...] = _dot(xb_ref[...], w_ref[...].astype(BF16))


def _inproj(x, w_in, wag, wa2, ba):
    t = x.shape[0]
    tm = _pick(t, (1024, 512, 256, 128))
    ncol = MAIN_COLS // COL_BLOCK
    return pl.pallas_call(
        _inproj_kernel,
        grid=(t // tm, ncol),
        in_specs=[
            pl.BlockSpec((tm, D_MODEL), lambda i, j: (i, 0)),
            pl.BlockSpec((D_MODEL, COL_BLOCK), lambda i, j: (0, j)),
            pl.BlockSpec((D_MODEL, LANES), lambda i, j: (0, 0)),
            pl.BlockSpec((LANES, GLA_QK_COLS), lambda i, j: (0, 0)),
            pl.BlockSpec((1, GLA_QK_COLS), lambda i, j: (0, 0)),
        ],
        out_specs=[
            pl.BlockSpec((tm, COL_BLOCK), lambda i, j: (i, j)),
            pl.BlockSpec((tm, GLA_QK_COLS), lambda i, j: (i, 0)),
        ],
        out_shape=[
            jax.ShapeDtypeStruct((t, MAIN_COLS), F32),
            jax.ShapeDtypeStruct((t, GLA_QK_COLS), F32),
        ],
        scratch_shapes=[pltpu.VMEM((tm, D_MODEL), BF16)],
        compiler_params=_params(("arbitrary", "arbitrary")),
        name="inproj",
    )(x, w_in, wag, wa2, ba)


def _half_variants(x):
    lane = lax.broadcasted_iota(jnp.int32, (x.shape[0], LANES), 1)
    lo = lane < HEAD_DIM
    out = []
    for g in range(N_KV_HEADS):
        chunk = x[:, (g // 2) * LANES:(g // 2 + 1) * LANES]
        swapped = pltpu.roll(chunk, HEAD_DIM, 1)
        u = g % 2
        for b in range(2):
            src = chunk if b == u else swapped
            keep = lo if b == 0 else jnp.logical_not(lo)
            out.append(jnp.where(keep, src, 0.0).astype(BF16))
    return out


def _swa_kernel(q_ref, kv_ref, c_ref, sa_ref, sb_ref, sink_ref, g_ref,
                oa_ref, nk_ref, nv_ref, kp_ref, vp_ref):
    i = pl.program_id(0)
    blk = q_ref.shape[0]
    c, sa, sb = c_ref[...], sa_ref[...], sb_ref[...]

    @pl.when(i == 0)
    def _():
        kp_ref[...] = jnp.zeros_like(kp_ref)
        vp_ref[...] = jnp.zeros_like(vp_ref)

    q = (_rope(q_ref[...], c, sa, sb) * (HEAD_DIM ** -0.5)).astype(BF16)
    kv = kv_ref[...]
    k = _rope(kv[:, :KV_COLS], c, sa, sb)
    v = kv[:, KV_COLS:]
    kc = _half_variants(k)
    vc = _half_variants(v)

    row = lax.broadcasted_iota(jnp.int32, (blk, 2 * blk), 0)
    col = lax.broadcasted_iota(jnp.int32, (blk, 2 * blk), 1)
    first_key = jnp.where(i > 0, 0, blk)
    mask = (col > row) & (col <= row + WINDOW) & (col >= first_key)

    pairs = []
    for p in range(N_HEADS // 2):
        qp = q[:, p * LANES:(p + 1) * LANES]
        acc = jnp.zeros((blk, LANES), F32)
        for b in range(2):
            h = 2 * p + b
            g = h // (N_HEADS // N_KV_HEADS)
            kcat = jnp.concatenate([kp_ref[2 * g + b], kc[2 * g + b]], axis=0)
            vcat = jnp.concatenate([vp_ref[2 * g + b], vc[2 * g + b]], axis=0)
            s = jnp.where(mask, _dot_nt(qp, kcat), -jnp.inf)
            sk = sink_ref[h]
            m = jnp.maximum(jnp.max(s, axis=1, keepdims=True), sk)
            e = jnp.exp(s - m)
            den = jnp.sum(e, axis=1, keepdims=True) + jnp.exp(sk - m)
            acc = acc + _dot((e / den).astype(BF16), vcat)
        pairs.append(acc)
    o = jnp.concatenate(pairs, axis=1)
    ms = jnp.mean(o * o, axis=1, keepdims=True)
    oa_ref[...] = (o * lax.rsqrt(ms + RMS_EPS) * g_ref[...]).astype(oa_ref.dtype)

    for n in range(2 * N_KV_HEADS):
        kp_ref[n] = kc[n]
        vp_ref[n] = vc[n]

    @pl.when(i == pl.num_programs(0) - 1)
    def _():
        nk_ref[...] = k
        nv_ref[...] = v


def _swa_prompt(h, c, sa, sb, sinks, g):
    t = h.shape[0]
    blk = WINDOW
    nb = t // blk
    return pl.pallas_call(
        _swa_kernel,
        grid=(nb,),
        in_specs=[
            pl.BlockSpec((blk, Q_COLS), lambda i: (i, 0)),
            pl.BlockSpec((blk, 2 * KV_COLS), lambda i: (i, Q_COLS // (2 * KV_COLS))),
            pl.BlockSpec((blk, LANES), lambda i: (i, 0)),
            pl.BlockSpec((blk, LANES), lambda i: (i, 0)),
            pl.BlockSpec((blk, LANES), lambda i: (i, 0)),
            pl.BlockSpec(memory_space=pltpu.SMEM),
            pl.BlockSpec((1, Q_COLS), lambda i: (0, 0)),
        ],
        out_specs=[
            pl.BlockSpec((blk, Q_COLS), lambda i: (i, 0)),
            pl.BlockSpec((blk, KV_COLS), lambda i: (0, 0)),
            pl.BlockSpec((blk, KV_COLS), lambda i: (0, 0)),
        ],
        out_shape=[
            jax.ShapeDtypeStruct((t, Q_COLS), BF16),
            jax.ShapeDtypeStruct((blk, KV_COLS), F32),
            jax.ShapeDtypeStruct((blk, KV_COLS), F32),
        ],
        scratch_shapes=[pltpu.VMEM((2 * N_KV_HEADS, blk, LANES), BF16),
                        pltpu.VMEM((2 * N_KV_HEADS, blk, LANES), BF16)],
        compiler_params=_params(("arbitrary",)),
        name="swa_prompt",
    )(h, h, c, sa, sb, sinks, g)


def _silu(x):
    return x * jax.nn.sigmoid(x)


def _gla_kernel(qg_ref, kg_ref, v1_ref, v2_ref, r1_ref, r2_ref, la_ref, gn_ref,
                og_ref, sfin_ref, s_ref):
    i = pl.program_id(0)
    rows = qg_ref.shape[0]
    ch = GLA_CHUNK
    nsub = ch // GLA_SUB

    @pl.when(i == 0)
    def _():
        s_ref[...] = jnp.zeros_like(s_ref)

    rr = lax.broadcasted_iota(jnp.int32, (ch, ch), 0)
    cc = lax.broadcasted_iota(jnp.int32, (ch, ch), 1)
    tri = jnp.where(rr >= cc, 1.0, 0.0).astype(BF16)
    pre = jnp.where(cc < (rr // GLA_SUB) * GLA_SUB, 1.0, 0.0).astype(BF16)
    mr = lax.broadcasted_iota(jnp.int32, (ch, nsub * ch), 0)
    mc = lax.broadcasted_iota(jnp.int32, (ch, nsub * ch), 1)
    amask = ((mc // ch) == (mr // GLA_SUB)) & ((mc % ch) <= mr)
    ones_t = jnp.ones((ch, LANES), BF16)

    def chunk(ci, carry):
        r0 = pl.multiple_of(ci * ch, ch)
        for hh in range(GLA_HEADS):
            dk = slice(hh * GLA_DK, (hh + 1) * GLA_DK)
            vref, rref = (v1_ref, r1_ref) if hh < 2 else (v2_ref, r2_ref)
            dv = slice((hh % 2) * GLA_DV, (hh % 2 + 1) * GLA_DV)
            q = qg_ref[pl.ds(r0, ch), dk] * (GLA_DK ** -0.5)
            k = kg_ref[pl.ds(r0, ch), dk]
            la = la_ref[pl.ds(r0, ch), dk]
            v = vref[pl.ds(r0, ch), dv]
            r = rref[pl.ds(r0, ch), dv]
            hi, mid, lo = _split3(la)
            b = _dot(tri, hi) + _dot(tri, mid) + _dot(tri, lo)
            bs = _dot(pre, hi) + _dot(pre, mid) + _dot(pre, lo)
            blast = b[ch - 1:ch, :]
            bl_col = _dot_tn(hi, ones_t) + _dot_tn(mid, ones_t) + _dot_tn(lo, ones_t)
            dec_col = jnp.exp(jnp.concatenate([bl_col, bl_col], axis=1))

            qt = (q * jnp.exp(b - bs)).astype(BF16)
            kparts = []
            for si in range(nsub):
                ref_row = bs[si * GLA_SUB:si * GLA_SUB + 1, :]
                kparts.append((k * jnp.exp(jnp.minimum(ref_row - b, EXP_CLAMP))).astype(BF16))
            kcat = jnp.concatenate(kparts, axis=0)
            a = jnp.where(amask, _dot_nt(qt, kcat), 0.0).astype(BF16)
            vb = v.astype(BF16)
            vrep = jnp.concatenate([vb] * nsub, axis=0)
            s_old = s_ref[hh]
            o = _dot(a, vrep) + _dot((q * jnp.exp(b)).astype(BF16), s_old.astype(BF16))
            kd = (k * jnp.exp(blast - b)).astype(BF16)
            s_ref[hh] = dec_col * s_old + _dot_tn(kd, vb)

            ms = jnp.mean(o * o, axis=1, keepdims=True)
            gn = gn_ref[:, hh * GLA_DV:(hh + 1) * GLA_DV]
            y = o * lax.rsqrt(ms + RMS_EPS) * gn
            og_ref[pl.ds(r0, ch), hh * GLA_DV:(hh + 1) * GLA_DV] = (y * _silu(r)).astype(og_ref.dtype)
        return carry

    lax.fori_loop(0, rows // ch, chunk, 0)

    @pl.when(i == pl.num_programs(0) - 1)
    def _():
        sfin_ref[...] = s_ref[...]


def _gla_prompt(h, la, gn):
    t = h.shape[0]
    rows = _pick(t, (512, 256, 128, 64))
    cb = COL_BLOCK
    base = (Q_COLS + 2 * KV_COLS) // cb
    spec = lambda j: pl.BlockSpec((rows, cb), lambda i: (i, j))
    return pl.pallas_call(
        _gla_kernel,
        grid=(t // rows,),
        in_specs=[spec(base), spec(base + 1), spec(base + 2), spec(base + 3), spec(base + 4), spec(base + 5),
                  pl.BlockSpec((rows, GLA_QK_COLS), lambda i: (i, 0)),
                  pl.BlockSpec((1, GLA_WIDTH), lambda i: (0, 0))],
        out_specs=[
            pl.BlockSpec((rows, GLA_WIDTH), lambda i: (i, 0)),
            pl.BlockSpec((GLA_HEADS, GLA_DK, GLA_DV), lambda i: (0, 0, 0)),
        ],
        out_shape=[
            jax.ShapeDtypeStruct((t, GLA_WIDTH), BF16),
            jax.ShapeDtypeStruct((GLA_HEADS, GLA_DK, GLA_DV), F32),
        ],
        scratch_shapes=[pltpu.VMEM((GLA_HEADS, GLA_DK, GLA_DV), F32)],
        compiler_params=_params(("arbitrary",)),
        name="gla_prompt",
    )(h, h, h, h, h, h, la, gn)


def _dup_heads(x):
    lane = lax.broadcasted_iota(jnp.int32, (x.shape[0], LANES), 1)
    lo = lane < HEAD_DIM
    out = []
    for g in range(N_KV_HEADS):
        chunk = x[:, (g // 2) * LANES:(g // 2 + 1) * LANES]
        swapped = pltpu.roll(chunk, HEAD_DIM, 1)
        out.append(jnp.where(lo, chunk, swapped) if g % 2 == 0 else jnp.where(lo, swapped, chunk))
    return out


def _swa_dec_kernel(q_ref, kv_ref, ck_ref, cv_ref, c_ref, sa_ref, sb_ref, sink_ref, g_ref, e_ref, et_ref,
                    oa_ref, nk_ref, nv_ref):
    win = ck_ref.shape[1]
    c, sa, sb = c_ref[...], sa_ref[...], sb_ref[...]
    q8 = jnp.broadcast_to(q_ref[0], (8, Q_COLS))
    kv8 = jnp.broadcast_to(kv_ref[0], (8, 2 * KV_COLS))
    q = _rope(q8, c, sa, sb) * (HEAD_DIM ** -0.5)
    knew = _rope(kv8[:, :KV_COLS], c, sa, sb)
    vnew = kv8[:, KV_COLS:]

    row = lax.broadcasted_iota(jnp.int32, (win, KV_COLS), 0)
    last = row == win - 1
    keys = jnp.where(last, jnp.broadcast_to(knew[0:1], (win, KV_COLS)), pltpu.roll(ck_ref[0], win - 1, 0))
    vals = jnp.where(last, jnp.broadcast_to(vnew[0:1], (win, KV_COLS)), pltpu.roll(cv_ref[0], win - 1, 0))
    nk_ref[0] = keys
    nv_ref[0] = vals

    qb = q.astype(BF16).astype(F32)
    kd = _dup_heads(keys.astype(BF16).astype(F32))
    vd = _dup_heads(vals.astype(BF16).astype(F32))
    npair = N_HEADS // 2
    prod = jnp.concatenate(
        [kd[p // 2] * jnp.broadcast_to(qb[0:1, p * LANES:(p + 1) * LANES], (win, LANES)) for p in range(npair)],
        axis=1)
    st = _dot(prod.astype(BF16), e_ref[...])
    sk = sink_ref[...]
    m = jnp.maximum(jnp.max(st, axis=0, keepdims=True), sk)
    e = jnp.exp(st - m)
    den = jnp.sum(e, axis=0, keepdims=True) + jnp.exp(sk - m)
    pe = _dot((e / den).astype(BF16), et_ref[...])
    vcat = jnp.concatenate([vd[p // 2] for p in range(npair)], axis=1)
    o = jnp.sum(pe * vcat, axis=0, keepdims=True)
    ms = jnp.mean(o * o, axis=1, keepdims=True)
    oa_ref[0] = (o * lax.rsqrt(ms + RMS_EPS) * g_ref[...]).astype(oa_ref.dtype)


def _swa_decode(h3, ck, cv, c, sa, sb, sinks, g, e, et):
    bsz = h3.shape[0]
    win = ck.shape[1]
    const = lambda shape: pl.BlockSpec(shape, lambda b: tuple(0 for _ in shape))
    return pl.pallas_call(
        _swa_dec_kernel,
        grid=(bsz,),
        in_specs=[
            pl.BlockSpec((1, 1, Q_COLS), lambda b: (b, 0, 0)),
            pl.BlockSpec((1, 1, 2 * KV_COLS), lambda b: (b, 0, Q_COLS // (2 * KV_COLS))),
            pl.BlockSpec((1, win, KV_COLS), lambda b: (b, 0, 0)),
            pl.BlockSpec((1, win, KV_COLS), lambda b: (b, 0, 0)),
            const((1, LANES)), const((1, LANES)), const((1, LANES)), const((1, LANES)),
            const((1, Q_COLS)), const((Q_COLS, LANES)), const((LANES, Q_COLS)),
        ],
        out_specs=[
            pl.BlockSpec((1, 1, Q_COLS), lambda b: (b, 0, 0)),
            pl.BlockSpec((1, win, KV_COLS), lambda b: (b, 0, 0)),
            pl.BlockSpec((1, win, KV_COLS), lambda b: (b, 0, 0)),
        ],
        out_shape=[
            jax.ShapeDtypeStruct((bsz, 1, Q_COLS), BF16),
            jax.ShapeDtypeStruct((bsz, win, KV_COLS), F32),
            jax.ShapeDtypeStruct((bsz, win, KV_COLS), F32),
        ],
        compiler_params=_params(("arbitrary",)),
        name="swa_decode",
    )(h3, h3, ck, cv, c, sa, sb, sinks, g, e, et)


def _rows8(rows):
    w = rows[0].shape[1]
    ridx = lax.broadcasted_iota(jnp.int32, (8, w), 0)
    out = jnp.zeros((8, w), F32)
    for n, r in enumerate(rows):
        out = jnp.where(ridx == n, jnp.broadcast_to(r, (8, w)), out)
    return out


def _gla_dec_kernel(qg_ref, kg_ref, v1_ref, v2_ref, r1_ref, r2_ref, la_ref, s_ref, gn_ref, og_ref, sn_ref):
    ones8 = jnp.where(lax.broadcasted_iota(jnp.int32, (8, GLA_DV), 0) < 3, 1.0, 0.0).astype(BF16)
    outs = []
    for hh in range(GLA_HEADS):
        dk = slice(hh * GLA_DK, (hh + 1) * GLA_DK)
        vref, rref = (v1_ref, r1_ref) if hh < 2 else (v2_ref, r2_ref)
        dv = slice((hh % 2) * GLA_DV, (hh % 2 + 1) * GLA_DV)
        q = qg_ref[0][:, dk] * (GLA_DK ** -0.5)
        k = kg_ref[0][:, dk]
        la = la_ref[0][:, dk]
        v = vref[0][:, dv]
        r = rref[0][:, dv]
        s_old = s_ref[0, hh]
        dec = jnp.exp(la)
        qb = q.astype(BF16).astype(F32)
        kb = k.astype(BF16).astype(F32)
        score = jnp.sum(qb * kb, axis=1, keepdims=True)
        q8 = _rows8([q * dec]).astype(BF16)
        o = score * v + _dot(q8, s_old.astype(BF16))[0:1, :]
        dhi, dmid, dlo = _split3(dec)
        d8 = _rows8([dhi.astype(F32), dmid.astype(F32), dlo.astype(F32)]).astype(BF16)
        dec_col = _dot_tn(d8, ones8)
        k8 = _rows8([k]).astype(BF16)
        v8 = _rows8([v]).astype(BF16)
        sn_ref[0, hh] = dec_col * s_old + _dot_tn(k8, v8)
        ms = jnp.mean(o * o, axis=1, keepdims=True)
        gn = gn_ref[:, hh * GLA_DV:(hh + 1) * GLA_DV]
        outs.append(o * lax.rsqrt(ms + RMS_EPS) * gn * _silu(r))
    og_ref[0] = jnp.concatenate(outs, axis=1).astype(og_ref.dtype)


def _gla_decode(h3, la3, s0, gn):
    bsz = h3.shape[0]
    cb = COL_BLOCK
    base = (Q_COLS + 2 * KV_COLS) // cb
    spec = lambda j: pl.BlockSpec((1, 1, cb), lambda b: (b, 0, j))
    return pl.pallas_call(
        _gla_dec_kernel,
        grid=(bsz,),
        in_specs=[spec(base), spec(base + 1), spec(base + 2), spec(base + 3), spec(base + 4), spec(base + 5),
                  pl.BlockSpec((1, 1, GLA_QK_COLS), lambda b: (b, 0, 0)),
                  pl.BlockSpec((1, GLA_HEADS, GLA_DK, GLA_DV), lambda b: (b, 0, 0, 0)),
                  pl.BlockSpec((1, GLA_WIDTH), lambda b: (0, 0))],
        out_specs=[
            pl.BlockSpec((1, 1, GLA_WIDTH), lambda b: (b, 0, 0)),
            pl.BlockSpec((1, GLA_HEADS, GLA_DK, GLA_DV), lambda b: (b, 0, 0, 0)),
        ],
        out_shape=[
            jax.ShapeDtypeStruct((bsz, 1, GLA_WIDTH), BF16),
            jax.ShapeDtypeStruct(s0.shape, s0.dtype),
        ],
        compiler_params=_params(("arbitrary",)),
        name="gla_decode",
    )(h3, h3, h3, h3, h3, h3, la3, s0, gn)


def _layernorm(v, g, b):
    mu = jnp.mean(v, axis=1, keepdims=True)
    d = v - mu
    var = jnp.mean(d * d, axis=1, keepdims=True)
    return d * lax.rsqrt(var + LN_EPS) * g + b


def _load_bf16(w_hbm, dst_ref, stage_ref, sem, chunk):
    for s in range(w_hbm.shape[0] // chunk):
        cp = pltpu.make_async_copy(w_hbm.at[pl.ds(s * chunk, chunk), :], stage_ref, sem)
        cp.start()
        cp.wait()
        dst_ref[pl.ds(s * chunk, chunk), :] = stage_ref[...].astype(BF16)


def _route(logits, bias):
    rows, ne = logits.shape
    per = ne // N_GROUPS
    ninf = -jnp.inf
    lane = lax.broadcasted_iota(jnp.int32, (rows, ne), 1)
    gid = lane // per
    scores = jax.nn.sigmoid(logits)
    biased = scores + bias
    gfull = jnp.zeros((rows, ne), F32)
    gcols = []
    for g in range(N_GROUPS):
        ing = gid == g
        xg = jnp.where(ing, biased, ninf)
        m1 = jnp.max(xg, axis=1, keepdims=True)
        cnt = jnp.sum(jnp.where(xg == m1, 1.0, 0.0), axis=1, keepdims=True)
        m2 = jnp.max(jnp.where(xg < m1, xg, ninf), axis=1, keepdims=True)
        gs = m1 + jnp.where(cnt >= 2.0, m1, m2)
        gcols.append(gs)
        gfull = jnp.where(ing, gs, gfull)
    rank = jnp.zeros((rows, ne), jnp.int32)
    for g in range(N_GROUPS):
        beats = (gcols[g] > gfull) | ((gcols[g] == gfull) & (gid > g))
        rank = rank + jnp.where(beats, 1, 0)
    masked = jnp.where(rank < TOPK_GROUPS, biased, ninf)

    lane_f = lane.astype(F32)
    out_lane = lax.broadcasted_iota(jnp.int32, (rows, LANES), 1)
    eout = jnp.zeros((rows, LANES), F32)
    gout = jnp.zeros((rows, LANES), F32)
    total = jnp.zeros((rows, 1), F32)
    chosen = jnp.zeros((rows, ne), F32)
    for kk in range(TOP_K):
        m = jnp.max(masked, axis=1, keepdims=True)
        idx = jnp.min(jnp.where(masked == m, lane_f, float(ne)), axis=1, keepdims=True)
        hit = lane_f == idx
        gk = jnp.sum(jnp.where(hit, scores, 0.0), axis=1, keepdims=True)
        masked = jnp.where(hit, ninf, masked)
        chosen = jnp.where(hit, 1.0, chosen)
        total = total + gk
        eout = jnp.where(out_lane == kk, idx, eout)
        gout = jnp.where(out_lane == kk, gk, gout)
    counts = jnp.sum(chosen, axis=0, keepdims=True)
    return eout.astype(jnp.int32), gout / total * ROUTED_SCALE, counts


def _outproj_kernel(x_ref, oa_ref, og_ref, wout_hbm, g1_ref, b1_ref, wr_ref, rb_ref,
                    x1_ref, eidx_ref, gate_ref, cnt_ref, wo_ref, stage_ref, sem):
    @pl.when(pl.program_id(0) == 0)
    def _():
        _load_bf16(wout_hbm, wo_ref, stage_ref, sem, stage_ref.shape[0])
        cnt_ref[...] = jnp.zeros_like(cnt_ref)

    half = oa_ref.shape[1]
    mix = _dot(oa_ref[...], wo_ref[pl.ds(0, half), :]) + _dot(og_ref[...], wo_ref[pl.ds(half, half), :])
    x1 = _layernorm(DEEPNORM_ALPHA * x_ref[...] + mix, g1_ref[...], b1_ref[...])
    x1_ref[...] = x1
    logits = _dot(x1.astype(BF16), wr_ref[...].astype(BF16))
    eidx, gates, counts = _route(logits, rb_ref[...])
    eidx_ref[...] = eidx
    gate_ref[...] = gates
    cnt_ref[...] += counts


def _outproj_router(x, oa, og, w_out, g1, b1, w_router, rbias):
    t = x.shape[0]
    tm = _pick(t, (384, 256, 128, 64, 8))
    ne = w_router.shape[1]
    const = lambda shape: pl.BlockSpec(shape, lambda i: tuple(0 for _ in shape))
    return pl.pallas_call(
        _outproj_kernel,
        grid=(t // tm,),
        in_specs=[
            pl.BlockSpec((tm, D_MODEL), lambda i: (i, 0)),
            pl.BlockSpec((tm, Q_COLS), lambda i: (i, 0)),
            pl.BlockSpec((tm, GLA_WIDTH), lambda i: (i, 0)),
            pl.BlockSpec(memory_space=pl.ANY),
            const((1, D_MODEL)), const((1, D_MODEL)), const((D_MODEL, ne)), const((1, ne)),
        ],
        out_specs=[
            pl.BlockSpec((tm, D_MODEL), lambda i: (i, 0)),
            pl.BlockSpec((tm, LANES), lambda i: (i, 0)),
            pl.BlockSpec((tm, LANES), lambda i: (i, 0)),
            const((1, ne)),
        ],
        out_shape=[
            jax.ShapeDtypeStruct((t, D_MODEL), F32),
            jax.ShapeDtypeStruct((t, LANES), jnp.int32),
            jax.ShapeDtypeStruct((t, LANES), F32),
            jax.ShapeDtypeStruct((1, ne), F32),
        ],
        scratch_shapes=[pltpu.VMEM((D_MODEL, D_MODEL), BF16),
                        pltpu.VMEM((512, D_MODEL), F32),
                        pltpu.SemaphoreType.DMA(())],
        compiler_params=_params(("arbitrary",)),
        name="outproj_router",
    )(x, oa, og, w_out, g1, b1, w_router, rbias)


FLAG_FIRST_OF_BLOCK = 1
FLAG_LAST_OF_BLOCK = 2
FLAG_NEW_EXPERT = 4
FLAG_HAS_NEXT_EXPERT = 8
FLAG_WEIGHT_SLOT = 16


def _moe_kernel(wblk, wexp, wlo, whi, wflag, wnext, nwork_ref,
                src_hbm, dst_hbm, x_hbm, wg_hbm, wu_hbm, wd_hbm, y_hbm,
                xbuf, ybuf, wgf, wuf, wdf, wgb, wub, wdb, srcs, dsts, gsem, ssem, isem, wsem):
    nblk = src_hbm.shape[0]
    blk = xbuf.shape[1]
    ff = wgb.shape[1]

    def src_copy(j, p):
        return pltpu.make_async_copy(src_hbm.at[j], srcs.at[p], isem.at[0, p])

    def dst_copy(j, p):
        return pltpu.make_async_copy(dst_hbm.at[j], dsts.at[p], isem.at[1, p])

    def gather_copy(tok, r, p):
        return pltpu.make_async_copy(x_hbm.at[pl.ds(tok, 1), :], xbuf.at[p, pl.ds(r, 1), :], gsem.at[p])

    def scatter_copy(dst, r, p):
        return pltpu.make_async_copy(ybuf.at[p, pl.ds(r, 1), :], y_hbm.at[pl.ds(dst, 1), :], ssem.at[p])

    def issue_gather(p):
        for r in range(blk):
            gather_copy(srcs[p, r], r, p).start()

    def wait_gather(p):
        for r in range(blk):
            gather_copy(0, r, p).wait()

    def issue_scatter(p):
        for r in range(blk):
            scatter_copy(dsts[p, r], r, p).start()

    def wait_scatter(p):
        for r in range(blk):
            scatter_copy(0, r, p).wait()

    def weight_copies(e, ws):
        return (pltpu.make_async_copy(wg_hbm.at[e], wgf.at[ws], wsem.at[ws, 0]),
                pltpu.make_async_copy(wu_hbm.at[e], wuf.at[ws], wsem.at[ws, 1]),
                pltpu.make_async_copy(wd_hbm.at[e], wdf.at[ws], wsem.at[ws, 2]))

    def cast_weights(ws):
        step = 256
        for c in range(0, wgb.shape[0], step):
            wgb[pl.ds(c, step), :] = wgf[ws, pl.ds(c, step), :].astype(BF16)
            wub[pl.ds(c, step), :] = wuf[ws, pl.ds(c, step), :].astype(BF16)
        step = 64
        for c in range(0, wdb.shape[0], step):
            wdb[pl.ds(c, step), :] = wdf[ws, pl.ds(c, step), :].astype(BF16)

    def block_start(b, p):
        @pl.when(b + 2 < nblk)
        def _():
            src_copy(b + 2, p).start()

        @pl.when(b + 1 < nblk)
        def _():
            dst_copy(b + 1, 1 - p).start()
            src_copy(b + 1, 1 - p).wait()
            issue_gather(1 - p)
        wait_gather(p)
        dst_copy(b, p).wait()

        @pl.when(b >= 2)
        def _():
            wait_scatter(p)

    src_copy(0, 0).start()
    dst_copy(0, 0).start()
    if nblk > 1:
        src_copy(1, 1).start()
    for cp in weight_copies(wexp[0], 0):
        cp.start()
    src_copy(0, 0).wait()
    issue_gather(0)

    def work(w, carry):
        b = wblk[w]
        fl = wflag[w]
        slot = b % 2
        first = (fl & FLAG_FIRST_OF_BLOCK) != 0
        last = (fl & FLAG_LAST_OF_BLOCK) != 0

        for p in range(2):
            @pl.when(first & (slot == p))
            def _(p=p):
                block_start(b, p)

        @pl.when((fl & FLAG_NEW_EXPERT) != 0)
        def _():
            for ws in range(2):
                @pl.when((fl & FLAG_WEIGHT_SLOT) == ws * FLAG_WEIGHT_SLOT)
                def _(ws=ws):
                    for cp in weight_copies(0, ws):
                        cp.wait()
                    cast_weights(ws)

                    @pl.when((fl & FLAG_HAS_NEXT_EXPERT) != 0)
                    def _():
                        for cp in weight_copies(wnext[w], 1 - ws):
                            cp.start()

        xb = xbuf[slot].astype(BF16)
        hact = _silu(_dot(xb, wgb[...])) * _dot(xb, wub[...])
        rows = lax.broadcasted_iota(jnp.int32, (blk, ff), 0)
        hact = jnp.where((rows >= wlo[w]) & (rows < whi[w]), hact, 0.0)
        y = _dot(hact.astype(BF16), wdb[...])

        @pl.when(first)
        def _():
            ybuf[slot] = y

        @pl.when(jnp.logical_not(first))
        def _():
            ybuf[slot] += y

        for p in range(2):
            @pl.when(last & (slot == p))
            def _(p=p):
                issue_scatter(p)
        return carry

    lax.fori_loop(0, nwork_ref[0], work, 0)
    if nblk > 1:
        wait_scatter((nblk - 2) % 2)
    wait_scatter((nblk - 1) % 2)


def _moe_experts(x1, plan, w_gate, w_up, w_down, n_out_rows):
    src, dst = plan[-2:]
    ne, d, ff = w_gate.shape
    anyspec = pl.BlockSpec(memory_space=pl.ANY)
    grid_spec = pltpu.PrefetchScalarGridSpec(
        num_scalar_prefetch=7,
        grid=(1,),
        in_specs=[anyspec] * 6,
        out_specs=anyspec,
        scratch_shapes=[
            pltpu.VMEM((2, MOE_BLK, d), F32),
            pltpu.VMEM((2, MOE_BLK, d), F32),
            pltpu.VMEM((2, d, ff), F32),
            pltpu.VMEM((2, d, ff), F32),
            pltpu.VMEM((2, ff, d), F32),
            pltpu.VMEM((d, ff), BF16),
            pltpu.VMEM((d, ff), BF16),
            pltpu.VMEM((ff, d), BF16),
            pltpu.SMEM((2, MOE_BLK), jnp.int32),
            pltpu.SMEM((2, MOE_BLK), jnp.int32),
            pltpu.SemaphoreType.DMA((2,)),
            pltpu.SemaphoreType.DMA((2,)),
            pltpu.SemaphoreType.DMA((2, 2)),
            pltpu.SemaphoreType.DMA((2, 3)),
        ],
    )
    return pl.pallas_call(
        _moe_kernel,
        grid_spec=grid_spec,
        out_shape=jax.ShapeDtypeStruct((n_out_rows, d), F32),
        compiler_params=_params(("arbitrary",)),
        name="moe_experts",
    )(*plan[:-2], src, dst, x1, w_gate, w_up, w_down)


def _dispatch_plan(eidx, counts_f):
    t = eidx.shape[0]
    ne = counts_f.shape[-1]
    n = t * TOP_K
    blk = MOE_BLK
    assert n % blk == 0
    nblk = n // blk
    nwmax = nblk + ne - 1
    bits = max(n - 1, 1).bit_length()
    flat = jnp.arange(n, dtype=jnp.int32)
    skeys = jnp.sort(eidx.reshape(-1) * (1 << bits) + flat)
    order = skeys & ((1 << bits) - 1)
    tok = order // TOP_K
    src = tok.reshape(nblk, blk)
    dst = ((order % TOP_K) * t + tok).reshape(nblk, blk)

    counts = counts_f.reshape(ne).astype(jnp.int32)
    ends = jnp.cumsum(counts)
    starts = ends - counts
    fb = starts // blk
    nit = jnp.where(counts > 0, (ends + blk - 1) // blk - fb, 0)
    icum = jnp.cumsum(nit)
    nwork = icum[-1]
    eids = jnp.arange(ne, dtype=jnp.int32)
    nxt = jnp.flip(lax.cummin(jnp.flip(jnp.where(counts > 0, eids, ne)), axis=0))
    nxt_after = jnp.concatenate([nxt[1:], jnp.full((1,), ne, jnp.int32)])
    used_rank = jnp.cumsum((counts > 0).astype(jnp.int32)) - 1

    w = jnp.arange(nwmax, dtype=jnp.int32)
    onehot = (w[:, None] >= (icum - nit)[None, :]) & (w[:, None] < icum[None, :])
    pick = lambda v: jnp.sum(jnp.where(onehot, v[None, :], 0), axis=1)
    we = pick(eids)
    wb = pick(fb) + w - pick(icum - nit)
    lo = jnp.clip(pick(starts) - wb * blk, 0, blk)
    hi = jnp.clip(pick(ends) - wb * blk, 0, blk)
    wn = pick(nxt_after)
    valid = w < nwork
    wb = jnp.where(valid, wb, nblk - 1)
    prev_b = jnp.concatenate([jnp.full((1,), -1, jnp.int32), wb[:-1]])
    next_b = jnp.concatenate([wb[1:], jnp.full((1,), -1, jnp.int32)])
    prev_e = jnp.concatenate([jnp.full((1,), -1, jnp.int32), we[:-1]])
    flags = (jnp.where(wb != prev_b, FLAG_FIRST_OF_BLOCK, 0)
             + jnp.where((wb != next_b) | (w == nwork - 1), FLAG_LAST_OF_BLOCK, 0)
             + jnp.where(we != prev_e, FLAG_NEW_EXPERT, 0)
             + jnp.where(wn < ne, FLAG_HAS_NEXT_EXPERT, 0)
             + (pick(used_rank) % 2) * FLAG_WEIGHT_SLOT)
    wn = jnp.minimum(wn, ne - 1)
    i32 = lambda v: v.astype(jnp.int32)
    return (i32(wb), i32(we), i32(lo), i32(hi), i32(flags), i32(wn), i32(nwork).reshape(1), src, dst)


def _final_kernel(*refs):
    x1_ref, gate_ref = refs[0], refs[1]
    y_refs = refs[2:2 + TOP_K]
    wsg_hbm, wsu_hbm, wsd_hbm, g2_ref, b2_ref, out_ref, wsg, wsu, wsd, stage_a, stage_b, sem = refs[2 + TOP_K:]

    @pl.when(pl.program_id(0) == 0)
    def _():
        _load_bf16(wsg_hbm, wsg, stage_a, sem, stage_a.shape[0])
        _load_bf16(wsu_hbm, wsu, stage_a, sem, stage_a.shape[0])
        _load_bf16(wsd_hbm, wsd, stage_b, sem, stage_b.shape[0])

    x1 = x1_ref[...]
    xb = x1.astype(BF16)
    hact = _silu(_dot(xb, wsg[...])) * _dot(xb, wsu[...])
    f = _dot(hact.astype(BF16), wsd[...])
    gates = gate_ref[...]
    routed = jnp.zeros_like(x1)
    for kk in range(TOP_K):
        routed = routed + y_refs[kk][...] * gates[:, kk:kk + 1]
    out_ref[...] = _layernorm(DEEPNORM_ALPHA * x1 + (f + routed), g2_ref[...], b2_ref[...])


def _final(x1, gates, ycomb, w_sg, w_su, w_sd, g2, b2):
    t = x1.shape[0]
    tm = _pick(t, (128, 64, 8))
    nt = t // tm
    ff = w_sg.shape[1]
    const = lambda shape: pl.BlockSpec(shape, lambda i: tuple(0 for _ in shape))
    yspec = lambda kk: pl.BlockSpec((tm, D_MODEL), lambda i: (kk * nt + i, 0))
    anyspec = pl.BlockSpec(memory_space=pl.ANY)
    return pl.pallas_call(
        _final_kernel,
        grid=(nt,),
        in_specs=[pl.BlockSpec((tm, D_MODEL), lambda i: (i, 0)),
                  pl.BlockSpec((tm, LANES), lambda i: (i, 0))]
                 + [yspec(kk) for kk in range(TOP_K)]
                 + [anyspec, anyspec, anyspec, const((1, D_MODEL)), const((1, D_MODEL))],
        out_specs=pl.BlockSpec((tm, D_MODEL), lambda i: (i, 0)),
        out_shape=jax.ShapeDtypeStruct((t, D_MODEL), F32),
        scratch_shapes=[pltpu.VMEM((D_MODEL, ff), BF16),
                        pltpu.VMEM((D_MODEL, ff), BF16),
                        pltpu.VMEM((ff, D_MODEL), BF16),
                        pltpu.VMEM((512, ff), F32),
                        pltpu.VMEM((128, D_MODEL), F32),
                        pltpu.SemaphoreType.DMA(())],
        compiler_params=_params(("arbitrary",)),
        name="shared_combine_ln",
    )(x1, gates, *([ycomb] * TOP_K), w_sg, w_su, w_sd, g2, b2)


def kernel(x_prompt, x_sample, cache_k_win, cache_v_win, state_gla, w_in, attn_sinks, attn_norm_g, w_gla_a2,
           b_gla_a, gla_norm_g, w_out, ln1_g, ln1_b, w_router, router_bias, w_exp_gate, w_exp_up, w_exp_down,
           w_sh_gate, w_sh_up, w_sh_down, ln2_g, ln2_b):
    assert w_in.shape[0] == DEPTH == 1 and x_prompt.shape[0] == 1 and x_sample.shape[1] == 1
    t = x_prompt.shape[1]
    bsz = x_sample.shape[0]
    win = cache_k_win.shape[2]
    assert t % WINDOW == 0 and win == WINDOW

    w_in0 = w_in[0]
    wag = jnp.pad(w_in0[:, MAIN_COLS:], ((0, 0), (0, LANES - GLA_RANK)))
    wa2 = jnp.pad(w_gla_a2[0], ((0, LANES - GLA_RANK), (0, 0)))
    ba = b_gla_a[0].reshape(1, GLA_QK_COLS)
    sinks = jnp.pad(attn_sinks[0].reshape(1, N_HEADS), ((0, 0), (0, LANES - N_HEADS)))
    ag = attn_norm_g[0].reshape(1, Q_COLS)
    gn = gla_norm_g[0].reshape(1, GLA_WIDTH)

    xp = x_prompt[0]
    hp, lap = _inproj(xp, w_in0, wag, wa2, ba)
    cp, sap, sbp = _rope_tables(jnp.arange(t, dtype=jnp.int32))
    oa_p, nk_p, nv_p = _swa_prompt(hp, cp, sap, sbp, attn_sinks[0], ag)
    og_p, s_p = _gla_prompt(hp, lap, gn)

    xs = x_sample[:, 0]
    hs, las = _inproj(xs, w_in0, wag, wa2, ba)
    cs, sas, sbs = _rope_tables(PAST_LEN + jnp.arange(1, dtype=jnp.int32))
    lane_head = jnp.arange(Q_COLS, dtype=jnp.int32)[:, None] // HEAD_DIM
    emat = (lane_head == jnp.arange(LANES, dtype=jnp.int32)[None, :]).astype(BF16)
    hs3 = hs.reshape(bsz, 1, MAIN_COLS)
    oa_s, nk_s, nv_s = _swa_decode(hs3, cache_k_win[0].reshape(bsz, win, KV_COLS),
                                   cache_v_win[0].reshape(bsz, win, KV_COLS), cs, sas, sbs, sinks, ag,
                                   emat, emat.T)
    og_s, s_s = _gla_decode(hs3, las.reshape(bsz, 1, GLA_QK_COLS), state_gla[0], gn)

    x_all = jnp.concatenate([xp, xs], axis=0)
    oa_all = jnp.concatenate([oa_p, oa_s.reshape(bsz, Q_COLS)], axis=0)
    og_all = jnp.concatenate([og_p, og_s.reshape(bsz, GLA_WIDTH)], axis=0)
    t_all = t + bsz
    x1, eidx, gates, counts = _outproj_router(x_all, oa_all, og_all, w_out[0], ln1_g[0].reshape(1, D_MODEL),
                                              ln1_b[0].reshape(1, D_MODEL), w_router[0],
                                              router_bias[0].reshape(1, -1))
    plan = _dispatch_plan(eidx[:, :TOP_K], counts)
    ycomb = _moe_experts(x1, plan, w_exp_gate[0], w_exp_up[0], w_exp_down[0], TOP_K * t_all)
    y_all = _final(x1, gates, ycomb, w_sh_gate[0], w_sh_up[0], w_sh_down[0],
                   ln2_g[0].reshape(1, D_MODEL), ln2_b[0].reshape(1, D_MODEL))

    y_prompt = y_all[:t].reshape(1, t, D_MODEL)
    y_sample = y_all[t:].reshape(bsz, 1, D_MODEL)
    new_k_prompt = nk_p.reshape(1, 1, win, N_KV_HEADS, HEAD_DIM)
    new_v_prompt = nv_p.reshape(1, 1, win, N_KV_HEADS, HEAD_DIM)
    new_s_prompt = s_p.reshape(1, 1, GLA_HEADS, GLA_DK, GLA_DV)
    new_k_sample = nk_s.reshape(1, bsz, win, N_KV_HEADS, HEAD_DIM)
    new_v_sample = nv_s.reshape(1, bsz, win, N_KV_HEADS, HEAD_DIM)
    new_s_sample = s_s.reshape(1, bsz, GLA_HEADS, GLA_DK, GLA_DV)
    return (y_prompt, y_sample, new_k_prompt, new_v_prompt, new_s_prompt,
            new_k_sample, new_v_sample, new_s_sample)
```

```python
import functools

import jax
import jax.numpy as jnp
from jax import lax
from jax.experimental import pallas as pl
from jax.experimental.pallas import tpu as pltpu

F32 = jnp.float32
BF16 = jnp.bfloat16

D_MODEL = 2048
PAST_LEN = 16384
HEAD_DIM = 64
N_HEADS = 16
N_KV_HEADS = 4
WINDOW = 128
ROPE_DIM = 16
ROPE_THETA = 500000.0
GLA_HEADS = 4
GLA_DV = 256
GLA_DK = 128
GLA_RANK = 16
GLA_TAU = 16.0
Q_COLS = N_HEADS * HEAD_DIM
KV_COLS = N_KV_HEADS * HEAD_DIM
GLA_QK_COLS = GLA_HEADS * GLA_DK
GLA_WIDTH = GLA_HEADS * GLA_DV
MAIN_COLS = Q_COLS + 2 * KV_COLS + 2 * GLA_QK_COLS + 2 * GLA_WIDTH
TOP_K = 8
N_GROUPS = 8
TOPK_GROUPS = 4
ROUTED_SCALE = 2.5
DEPTH = 1
DEEPNORM_ALPHA = (2 * DEPTH) ** 0.25
LN_EPS = 1e-5
RMS_EPS = 1e-6

LANES = 128
COL_BLOCK = 512
GLA_CHUNK = 64
GLA_SUB = 16
EXP_CLAMP = 80.0
ROW_TILE = 128
MOE_BLK = 128
MOE_RING = 3
VMEM_LIMIT = 56 * 1024 * 1024


def _pick(n, cands):
    for c in cands:
        if n % c == 0:
            return c
    raise ValueError(f"no tile in {cands} divides {n}")


def _params(sem):
    return pltpu.CompilerParams(dimension_semantics=sem, vmem_limit_bytes=VMEM_LIMIT)


def _dot(a, b):
    return jnp.dot(a, b, preferred_element_type=F32)


def _dot_nt(a, b):
    return lax.dot_general(a, b, (((1,), (1,)), ((), ())), preferred_element_type=F32)


def _dot_tn(a, b):
    return lax.dot_general(a, b, (((0,), (0,)), ((), ())), preferred_element_type=F32)


def _split3(x):
    hi = x.astype(BF16)
    r1 = x - hi.astype(F32)
    mid = r1.astype(BF16)
    lo = (r1 - mid.astype(F32)).astype(BF16)
    return hi, mid, lo


def _tile_lanes(t, width):
    reps = width // t.shape[-1]
    return t if reps == 1 else jnp.concatenate([t] * reps, axis=-1)


def _rope(x, c, sa, sb):
    w = x.shape[-1]
    return (x * _tile_lanes(c, w) + pltpu.roll(x, 8, 1) * _tile_lanes(sa, w)
            + pltpu.roll(x, w - 8, 1) * _tile_lanes(sb, w))


def _rope_tables(pos):
    half = ROPE_DIM // 2
    inv = 1.0 / (ROPE_THETA ** (jnp.arange(0, ROPE_DIM, 2, dtype=F32) / ROPE_DIM))
    ang = pos.astype(F32)[:, None] * inv[None, :]
    cos, sin = jnp.cos(ang), jnp.sin(ang)
    n = pos.shape[0]
    rest = HEAD_DIM - ROPE_DIM
    c = jnp.concatenate([cos, cos, jnp.ones((n, rest), F32)], axis=1)
    sa = jnp.concatenate([jnp.zeros((n, half), F32), sin, jnp.zeros((n, rest), F32)], axis=1)
    sb = jnp.concatenate([-sin, jnp.zeros((n, half + rest), F32)], axis=1)
    rep = LANES // HEAD_DIM
    return jnp.tile(c, (1, rep)), jnp.tile(sa, (1, rep)), jnp.tile(sb, (1, rep))


def _inproj_kernel(x_ref, w_ref, wag_ref, wa2_ref, ba_ref, h_ref, la_ref, xb_ref):
    @pl.when(pl.program_id(1) == 0)
    def _():
        xb = x_ref[...].astype(BF16)
        xb_ref[...] = xb
        ag = _dot(xb, wag_ref[...].astype(BF16))
        z = _dot(ag.astype(BF16), wa2_ref[...].astype(BF16)) + ba_ref[...]
        la_ref[...] = (jnp.minimum(z, 0.0) - jnp.log1p(jnp.exp(-jnp.abs(z)))) * (1.0 / GLA_TAU)

    h_ref[...] = _dot(xb_ref[...], w_ref[...])


def _inproj(x, w_in, wag, wa2, ba):
    t = x.shape[0]
    tm = _pick(t, (1024, 512, 256, 128))
    ncol = MAIN_COLS // COL_BLOCK
    return pl.pallas_call(
        _inproj_kernel,
        grid=(t // tm, ncol),
        in_specs=[
            pl.BlockSpec((tm, D_MODEL), lambda i, j: (i, 0)),
            pl.BlockSpec((D_MODEL, COL_BLOCK), lambda i, j: (0, j)),
            pl.BlockSpec((D_MODEL, LANES), lambda i, j: (0, 0)),
            pl.BlockSpec((LANES, GLA_QK_COLS), lambda i, j: (0, 0)),
            pl.BlockSpec((1, GLA_QK_COLS), lambda i, j: (0, 0)),
        ],
        out_specs=[
            pl.BlockSpec((tm, COL_BLOCK), lambda i, j: (i, j)),
            pl.BlockSpec((tm, GLA_QK_COLS), lambda i, j: (i, 0)),
        ],
        out_shape=[
            jax.ShapeDtypeStruct((t, MAIN_COLS), F32),
            jax.ShapeDtypeStruct((t, GLA_QK_COLS), F32),
        ],
        scratch_shapes=[pltpu.VMEM((tm, D_MODEL), BF16)],
        compiler_params=_params(("arbitrary", "arbitrary")),
        name="inproj",
    )(x, w_in, wag, wa2, ba)


def _half_variants(x):
    lane = lax.broadcasted_iota(jnp.int32, (x.shape[0], LANES), 1)
    lo = lane < HEAD_DIM
    out = []
    for g in range(N_KV_HEADS):
        chunk = x[:, (g // 2) * LANES:(g // 2 + 1) * LANES]
        swapped = pltpu.roll(chunk, HEAD_DIM, 1)
        u = g % 2
        for b in range(2):
            src = chunk if b == u else swapped
            keep = lo if b == 0 else jnp.logical_not(lo)
            out.append(jnp.where(keep, src, 0.0).astype(BF16))
    return out


def _swa_kernel(q_ref, kv_ref, c_ref, sa_ref, sb_ref, sink_ref, g_ref,
                oa_ref, nk_ref, nv_ref, kp_ref, vp_ref):
    i = pl.program_id(0)
    blk = q_ref.shape[0]
    c, sa, sb = c_ref[...], sa_ref[...], sb_ref[...]

    @pl.when(i == 0)
    def _():
        kp_ref[...] = jnp.zeros_like(kp_ref)
        vp_ref[...] = jnp.zeros_like(vp_ref)

    q = (_rope(q_ref[...], c, sa, sb) * (HEAD_DIM ** -0.5)).astype(BF16)
    kv = kv_ref[...]
    k = _rope(kv[:, :KV_COLS], c, sa, sb)
    v = kv[:, KV_COLS:]
    kc = _half_variants(k)
    vc = _half_variants(v)

    row = lax.broadcasted_iota(jnp.int32, (blk, 2 * blk), 0)
    col = lax.broadcasted_iota(jnp.int32, (blk, 2 * blk), 1)
    first_key = jnp.where(i > 0, 0, blk)
    mask = (col > row) & (col <= row + WINDOW) & (col >= first_key)

    pairs = []
    for p in range(N_HEADS // 2):
        qp = q[:, p * LANES:(p + 1) * LANES]
        acc = jnp.zeros((blk, LANES), F32)
        for b in range(2):
            h = 2 * p + b
            g = h // (N_HEADS // N_KV_HEADS)
            kcat = jnp.concatenate([kp_ref[2 * g + b], kc[2 * g + b]], axis=0)
            vcat = jnp.concatenate([vp_ref[2 * g + b], vc[2 * g + b]], axis=0)
            s = jnp.where(mask, _dot_nt(qp, kcat), -jnp.inf)
            sk = sink_ref[h]
            m = jnp.maximum(jnp.max(s, axis=1, keepdims=True), sk)
            e = jnp.exp(s - m)
            den = jnp.sum(e, axis=1, keepdims=True) + jnp.exp(sk - m)
            acc = acc + _dot((e / den).astype(BF16), vcat)
        pairs.append(acc)
    o = jnp.concatenate(pairs, axis=1)
    ms = jnp.mean(o * o, axis=1, keepdims=True)
    oa_ref[...] = (o * lax.rsqrt(ms + RMS_EPS) * g_ref[...]).astype(oa_ref.dtype)

    for n in range(2 * N_KV_HEADS):
        kp_ref[n] = kc[n]
        vp_ref[n] = vc[n]

    @pl.when(i == pl.num_programs(0) - 1)
    def _():
        nk_ref[...] = k
        nv_ref[...] = v


def _swa_prompt(h, c, sa, sb, sinks, g):
    t = h.shape[0]
    blk = WINDOW
    nb = t // blk
    return pl.pallas_call(
        _swa_kernel,
        grid=(nb,),
        in_specs=[
            pl.BlockSpec((blk, Q_COLS), lambda i: (i, 0)),
            pl.BlockSpec((blk, 2 * KV_COLS), lambda i: (i, Q_COLS // (2 * KV_COLS))),
            pl.BlockSpec((blk, LANES), lambda i: (i, 0)),
            pl.BlockSpec((blk, LANES), lambda i: (i, 0)),
            pl.BlockSpec((blk, LANES), lambda i: (i, 0)),
            pl.BlockSpec(memory_space=pltpu.SMEM),
            pl.BlockSpec((1, Q_COLS), lambda i: (0, 0)),
        ],
        out_specs=[
            pl.BlockSpec((blk, Q_COLS), lambda i: (i, 0)),
            pl.BlockSpec((blk, KV_COLS), lambda i: (0, 0)),
            pl.BlockSpec((blk, KV_COLS), lambda i: (0, 0)),
        ],
        out_shape=[
            jax.ShapeDtypeStruct((t, Q_COLS), BF16),
            jax.ShapeDtypeStruct((blk, KV_COLS), F32),
            jax.ShapeDtypeStruct((blk, KV_COLS), F32),
        ],
        scratch_shapes=[pltpu.VMEM((2 * N_KV_HEADS, blk, LANES), BF16),
                        pltpu.VMEM((2 * N_KV_HEADS, blk, LANES), BF16)],
        compiler_params=_params(("arbitrary",)),
        name="swa_prompt",
    )(h, h, c, sa, sb, sinks, g)


def _silu(x):
    return x * jax.nn.sigmoid(x)


def _gla_kernel(qg_ref, kg_ref, v1_ref, v2_ref, r1_ref, r2_ref, la_ref, gn_ref,
                og_ref, sfin_ref, s_ref):
    i = pl.program_id(0)
    rows = qg_ref.shape[0]
    ch = GLA_CHUNK
    nsub = ch // GLA_SUB

    @pl.when(i == 0)
    def _():
        s_ref[...] = jnp.zeros_like(s_ref)

    rr = lax.broadcasted_iota(jnp.int32, (ch, ch), 0)
    cc = lax.broadcasted_iota(jnp.int32, (ch, ch), 1)
    tri = jnp.where(rr >= cc, 1.0, 0.0).astype(BF16)
    pre = jnp.where(cc < (rr // GLA_SUB) * GLA_SUB, 1.0, 0.0).astype(BF16)
    mr = lax.broadcasted_iota(jnp.int32, (ch, nsub * ch), 0)
    mc = lax.broadcasted_iota(jnp.int32, (ch, nsub * ch), 1)
    amask = ((mc // ch) == (mr // GLA_SUB)) & ((mc % ch) <= mr)
    ones_t = jnp.ones((ch, LANES), BF16)

    def chunk(ci, carry):
        r0 = pl.multiple_of(ci * ch, ch)
        for hh in range(GLA_HEADS):
            dk = slice(hh * GLA_DK, (hh + 1) * GLA_DK)
            vref, rref = (v1_ref, r1_ref) if hh < 2 else (v2_ref, r2_ref)
            dv = slice((hh % 2) * GLA_DV, (hh % 2 + 1) * GLA_DV)
            q = qg_ref[pl.ds(r0, ch), dk] * (GLA_DK ** -0.5)
            k = kg_ref[pl.ds(r0, ch), dk]
            la = la_ref[pl.ds(r0, ch), dk]
            v = vref[pl.ds(r0, ch), dv]
            r = rref[pl.ds(r0, ch), dv]
            hi, mid, lo = _split3(la)
            b = _dot(tri, hi) + _dot(tri, mid) + _dot(tri, lo)
            bs = _dot(pre, hi) + _dot(pre, mid) + _dot(pre, lo)
            blast = b[ch - 1:ch, :]
            bl_col = _dot_tn(hi, ones_t) + _dot_tn(mid, ones_t) + _dot_tn(lo, ones_t)
            dec_col = jnp.exp(jnp.concatenate([bl_col, bl_col], axis=1))

            qt = (q * jnp.exp(b - bs)).astype(BF16)
            kparts = []
            for si in range(nsub):
                ref_row = bs[si * GLA_SUB:si * GLA_SUB + 1, :]
                kparts.append((k * jnp.exp(jnp.minimum(ref_row - b, EXP_CLAMP))).astype(BF16))
            kcat = jnp.concatenate(kparts, axis=0)
            a = jnp.where(amask, _dot_nt(qt, kcat), 0.0).astype(BF16)
            vb = v.astype(BF16)
            vrep = jnp.concatenate([vb] * nsub, axis=0)
            s_old = s_ref[hh]
            o = _dot(a, vrep) + _dot((q * jnp.exp(b)).astype(BF16), s_old.astype(BF16))
            kd = (k * jnp.exp(blast - b)).astype(BF16)
            s_ref[hh] = dec_col * s_old + _dot_tn(kd, vb)

            ms = jnp.mean(o * o, axis=1, keepdims=True)
            gn = gn_ref[:, hh * GLA_DV:(hh + 1) * GLA_DV]
            y = o * lax.rsqrt(ms + RMS_EPS) * gn
            og_ref[pl.ds(r0, ch), hh * GLA_DV:(hh + 1) * GLA_DV] = (y * _silu(r)).astype(og_ref.dtype)
        return carry

    lax.fori_loop(0, rows // ch, chunk, 0)

    @pl.when(i == pl.num_programs(0) - 1)
    def _():
        sfin_ref[...] = s_ref[...]


def _gla_prompt(h, la, gn):
    t = h.shape[0]
    rows = _pick(t, (512, 256, 128, 64))
    cb = COL_BLOCK
    base = (Q_COLS + 2 * KV_COLS) // cb
    spec = lambda j: pl.BlockSpec((rows, cb), lambda i: (i, j))
    return pl.pallas_call(
        _gla_kernel,
        grid=(t // rows,),
        in_specs=[spec(base), spec(base + 1), spec(base + 2), spec(base + 3), spec(base + 4), spec(base + 5),
                  pl.BlockSpec((rows, GLA_QK_COLS), lambda i: (i, 0)),
                  pl.BlockSpec((1, GLA_WIDTH), lambda i: (0, 0))],
        out_specs=[
            pl.BlockSpec((rows, GLA_WIDTH), lambda i: (i, 0)),
            pl.BlockSpec((GLA_HEADS, GLA_DK, GLA_DV), lambda i: (0, 0, 0)),
        ],
        out_shape=[
            jax.ShapeDtypeStruct((t, GLA_WIDTH), BF16),
            jax.ShapeDtypeStruct((GLA_HEADS, GLA_DK, GLA_DV), F32),
        ],
        scratch_shapes=[pltpu.VMEM((GLA_HEADS, GLA_DK, GLA_DV), F32)],
        compiler_params=_params(("arbitrary",)),
        name="gla_prompt",
    )(h, h, h, h, h, h, la, gn)


def _dup_heads(x):
    lane = lax.broadcasted_iota(jnp.int32, (x.shape[0], LANES), 1)
    lo = lane < HEAD_DIM
    out = []
    for g in range(N_KV_HEADS):
        chunk = x[:, (g // 2) * LANES:(g // 2 + 1) * LANES]
        swapped = pltpu.roll(chunk, HEAD_DIM, 1)
        out.append(jnp.where(lo, chunk, swapped) if g % 2 == 0 else jnp.where(lo, swapped, chunk))
    return out


def _swa_dec_kernel(q_ref, kv_ref, ck_ref, cv_ref, c_ref, sa_ref, sb_ref, sink_ref, g_ref, e_ref, et_ref,
                    oa_ref, nk_ref, nv_ref):
    win = ck_ref.shape[1]
    c, sa, sb = c_ref[...], sa_ref[...], sb_ref[...]
    q8 = jnp.broadcast_to(q_ref[0], (8, Q_COLS))
    kv8 = jnp.broadcast_to(kv_ref[0], (8, 2 * KV_COLS))
    q = _rope(q8, c, sa, sb) * (HEAD_DIM ** -0.5)
    knew = _rope(kv8[:, :KV_COLS], c, sa, sb)
    vnew = kv8[:, KV_COLS:]

    row = lax.broadcasted_iota(jnp.int32, (win, KV_COLS), 0)
    last = row == win - 1
    keys = jnp.where(last, jnp.broadcast_to(knew[0:1], (win, KV_COLS)), pltpu.roll(ck_ref[0], win - 1, 0))
    vals = jnp.where(last, jnp.broadcast_to(vnew[0:1], (win, KV_COLS)), pltpu.roll(cv_ref[0], win - 1, 0))
    nk_ref[0] = keys
    nv_ref[0] = vals

    qb = q.astype(BF16).astype(F32)
    kd = _dup_heads(keys.astype(BF16).astype(F32))
    vd = _dup_heads(vals.astype(BF16).astype(F32))
    npair = N_HEADS // 2
    prod = jnp.concatenate(
        [kd[p // 2] * jnp.broadcast_to(qb[0:1, p * LANES:(p + 1) * LANES], (win, LANES)) for p in range(npair)],
        axis=1)
    st = _dot(prod.astype(BF16), e_ref[...])
    sk = sink_ref[...]
    m = jnp.maximum(jnp.max(st, axis=0, keepdims=True), sk)
    e = jnp.exp(st - m)
    den = jnp.sum(e, axis=0, keepdims=True) + jnp.exp(sk - m)
    pe = _dot((e / den).astype(BF16), et_ref[...])
    vcat = jnp.concatenate([vd[p // 2] for p in range(npair)], axis=1)
    o = jnp.sum(pe * vcat, axis=0, keepdims=True)
    ms = jnp.mean(o * o, axis=1, keepdims=True)
    oa_ref[0] = (o * lax.rsqrt(ms + RMS_EPS) * g_ref[...]).astype(oa_ref.dtype)


def _swa_decode(h3, ck, cv, c, sa, sb, sinks, g, e, et):
    bsz = h3.shape[0]
    win = ck.shape[1]
    const = lambda shape: pl.BlockSpec(shape, lambda b: tuple(0 for _ in shape))
    return pl.pallas_call(
        _swa_dec_kernel,
        grid=(bsz,),
        in_specs=[
            pl.BlockSpec((1, 1, Q_COLS), lambda b: (b, 0, 0)),
            pl.BlockSpec((1, 1, 2 * KV_COLS), lambda b: (b, 0, Q_COLS // (2 * KV_COLS))),
            pl.BlockSpec((1, win, KV_COLS), lambda b: (b, 0, 0)),
            pl.BlockSpec((1, win, KV_COLS), lambda b: (b, 0, 0)),
            const((1, LANES)), const((1, LANES)), const((1, LANES)), const((1, LANES)),
            const((1, Q_COLS)), const((Q_COLS, LANES)), const((LANES, Q_COLS)),
        ],
        out_specs=[
            pl.BlockSpec((1, 1, Q_COLS), lambda b: (b, 0, 0)),
            pl.BlockSpec((1, win, KV_COLS), lambda b: (b, 0, 0)),
            pl.BlockSpec((1, win, KV_COLS), lambda b: (b, 0, 0)),
        ],
        out_shape=[
            jax.ShapeDtypeStruct((bsz, 1, Q_COLS), BF16),
            jax.ShapeDtypeStruct((bsz, win, KV_COLS), F32),
            jax.ShapeDtypeStruct((bsz, win, KV_COLS), F32),
        ],
        compiler_params=_params(("arbitrary",)),
        name="swa_decode",
    )(h3, h3, ck, cv, c, sa, sb, sinks, g, e, et)


def _rows8(rows):
    w = rows[0].shape[1]
    ridx = lax.broadcasted_iota(jnp.int32, (8, w), 0)
    out = jnp.zeros((8, w), F32)
    for n, r in enumerate(rows):
        out = jnp.where(ridx == n, jnp.broadcast_to(r, (8, w)), out)
    return out


def _gla_dec_kernel(qg_ref, kg_ref, v1_ref, v2_ref, r1_ref, r2_ref, la_ref, s_ref, gn_ref, og_ref, sn_ref):
    ones8 = jnp.where(lax.broadcasted_iota(jnp.int32, (8, GLA_DV), 0) < 3, 1.0, 0.0).astype(BF16)
    outs = []
    for hh in range(GLA_HEADS):
        dk = slice(hh * GLA_DK, (hh + 1) * GLA_DK)
        vref, rref = (v1_ref, r1_ref) if hh < 2 else (v2_ref, r2_ref)
        dv = slice((hh % 2) * GLA_DV, (hh % 2 + 1) * GLA_DV)
        q = qg_ref[0][:, dk] * (GLA_DK ** -0.5)
        k = kg_ref[0][:, dk]
        la = la_ref[0][:, dk]
        v = vref[0][:, dv]
        r = rref[0][:, dv]
        s_old = s_ref[0, hh]
        dec = jnp.exp(la)
        qb = q.astype(BF16).astype(F32)
        kb = k.astype(BF16).astype(F32)
        score = jnp.sum(qb * kb, axis=1, keepdims=True)
        q8 = _rows8([q * dec]).astype(BF16)
        o = score * v + _dot(q8, s_old.astype(BF16))[0:1, :]
        dhi, dmid, dlo = _split3(dec)
        d8 = _rows8([dhi.astype(F32), dmid.astype(F32), dlo.astype(F32)]).astype(BF16)
        dec_col = _dot_tn(d8, ones8)
        k8 = _rows8([k]).astype(BF16)
        v8 = _rows8([v]).astype(BF16)
        sn_ref[0, hh] = dec_col * s_old + _dot_tn(k8, v8)
        ms = jnp.mean(o * o, axis=1, keepdims=True)
        gn = gn_ref[:, hh * GLA_DV:(hh + 1) * GLA_DV]
        outs.append(o * lax.rsqrt(ms + RMS_EPS) * gn * _silu(r))
    og_ref[0] = jnp.concatenate(outs, axis=1).astype(og_ref.dtype)


def _gla_decode(h3, la3, s0, gn):
    bsz = h3.shape[0]
    cb = COL_BLOCK
    base = (Q_COLS + 2 * KV_COLS) // cb
    spec = lambda j: pl.BlockSpec((1, 1, cb), lambda b: (b, 0, j))
    return pl.pallas_call(
        _gla_dec_kernel,
        grid=(bsz,),
        in_specs=[spec(base), spec(base + 1), spec(base + 2), spec(base + 3), spec(base + 4), spec(base + 5),
                  pl.BlockSpec((1, 1, GLA_QK_COLS), lambda b: (b, 0, 0)),
                  pl.BlockSpec((1, GLA_HEADS, GLA_DK, GLA_DV), lambda b: (b, 0, 0, 0)),
                  pl.BlockSpec((1, GLA_WIDTH), lambda b: (0, 0))],
        out_specs=[
            pl.BlockSpec((1, 1, GLA_WIDTH), lambda b: (b, 0, 0)),
            pl.BlockSpec((1, GLA_HEADS, GLA_DK, GLA_DV), lambda b: (b, 0, 0, 0)),
        ],
        out_shape=[
            jax.ShapeDtypeStruct((bsz, 1, GLA_WIDTH), BF16),
            jax.ShapeDtypeStruct(s0.shape, s0.dtype),
        ],
        compiler_params=_params(("arbitrary",)),
        name="gla_decode",
    )(h3, h3, h3, h3, h3, h3, la3, s0, gn)


def _layernorm(v, g, b):
    mu = jnp.mean(v, axis=1, keepdims=True)
    d = v - mu
    var = jnp.mean(d * d, axis=1, keepdims=True)
    return d * lax.rsqrt(var + LN_EPS) * g + b


def _load_bf16(w_hbm, dst_ref, stage_ref, sem, chunk):
    for s in range(w_hbm.shape[0] // chunk):
        cp = pltpu.make_async_copy(w_hbm.at[pl.ds(s * chunk, chunk), :], stage_ref, sem)
        cp.start()
        cp.wait()
        dst_ref[pl.ds(s * chunk, chunk), :] = stage_ref[...].astype(BF16)


def _route(logits, bias):
    rows, ne = logits.shape
    per = ne // N_GROUPS
    ninf = -jnp.inf
    lane = lax.broadcasted_iota(jnp.int32, (rows, ne), 1)
    gid = lane // per
    scores = jax.nn.sigmoid(logits)
    biased = scores + bias
    gfull = jnp.zeros((rows, ne), F32)
    gcols = []
    for g in range(N_GROUPS):
        ing = gid == g
        xg = jnp.where(ing, biased, ninf)
        m1 = jnp.max(xg, axis=1, keepdims=True)
        cnt = jnp.sum(jnp.where(xg == m1, 1.0, 0.0), axis=1, keepdims=True)
        m2 = jnp.max(jnp.where(xg < m1, xg, ninf), axis=1, keepdims=True)
        gs = m1 + jnp.where(cnt >= 2.0, m1, m2)
        gcols.append(gs)
        gfull = jnp.where(ing, gs, gfull)
    rank = jnp.zeros((rows, ne), jnp.int32)
    for g in range(N_GROUPS):
        beats = (gcols[g] > gfull) | ((gcols[g] == gfull) & (gid > g))
        rank = rank + jnp.where(beats, 1, 0)
    masked = jnp.where(rank < TOPK_GROUPS, biased, ninf)

    lane_f = lane.astype(F32)
    out_lane = lax.broadcasted_iota(jnp.int32, (rows, LANES), 1)
    eout = jnp.zeros((rows, LANES), F32)
    gout = jnp.zeros((rows, LANES), F32)
    total = jnp.zeros((rows, 1), F32)
    chosen = jnp.zeros((rows, ne), F32)
    for kk in range(TOP_K):
        m = jnp.max(masked, axis=1, keepdims=True)
        idx = jnp.min(jnp.where(masked == m, lane_f, float(ne)), axis=1, keepdims=True)
        hit = lane_f == idx
        gk = jnp.sum(jnp.where(hit, scores, 0.0), axis=1, keepdims=True)
        masked = jnp.where(hit, ninf, masked)
        chosen = jnp.where(hit, 1.0, chosen)
        total = total + gk
        eout = jnp.where(out_lane == kk, idx, eout)
        gout = jnp.where(out_lane == kk, gk, gout)
    counts = jnp.sum(chosen, axis=0, keepdims=True)
    return eout.astype(jnp.int32), gout / total * ROUTED_SCALE, counts


def _outproj_kernel(ntp, xp_ref, xs_ref, oap_ref, oas_ref, ogp_ref, ogs_ref, wout_hbm, g1_ref, b1_ref, wr_ref,
                    rb_ref, x1_ref, eidx_ref, gate_ref, cnt_ref, wo_ref, stage_ref, sem):
    @pl.when(pl.program_id(0) == 0)
    def _():
        _load_bf16(wout_hbm, wo_ref, stage_ref, sem, stage_ref.shape[0])
        cnt_ref[...] = jnp.zeros_like(cnt_ref)

    is_prompt = pl.program_id(0) < ntp
    x = jnp.where(is_prompt, xp_ref[...], xs_ref[...])
    oa = jnp.where(is_prompt, oap_ref[...], oas_ref[...])
    og = jnp.where(is_prompt, ogp_ref[...], ogs_ref[...])
    half = oa.shape[1]
    mix = _dot(oa, wo_ref[pl.ds(0, half), :]) + _dot(og, wo_ref[pl.ds(half, half), :])
    x1 = _layernorm(DEEPNORM_ALPHA * x + mix, g1_ref[...], b1_ref[...])
    x1_ref[...] = x1
    logits = _dot(x1.astype(BF16), wr_ref[...].astype(BF16))
    eidx, gates, counts = _route(logits, rb_ref[...])
    eidx_ref[...] = eidx
    gate_ref[...] = gates
    cnt_ref[...] += counts


def _outproj_router(xp, xs, oap, oas, ogp, ogs, w_out, g1, b1, w_router, rbias):
    tm = ROW_TILE
    tp, ts = xp.shape[0], xs.shape[0]
    assert tp % tm == 0 and ts % tm == 0
    ntp = tp // tm
    t = tp + ts
    ne = w_router.shape[1]
    const = lambda shape: pl.BlockSpec(shape, lambda i: tuple(0 for _ in shape))
    prompt = lambda width: pl.BlockSpec((tm, width), lambda i: (jnp.minimum(i, ntp - 1), 0))
    sample = lambda width: pl.BlockSpec((tm, width), lambda i: (jnp.maximum(i - ntp, 0), 0))
    return pl.pallas_call(
        functools.partial(_outproj_kernel, ntp),
        grid=(t // tm,),
        in_specs=[
            prompt(D_MODEL), sample(D_MODEL), prompt(Q_COLS), sample(Q_COLS), prompt(GLA_WIDTH), sample(GLA_WIDTH),
            pl.BlockSpec(memory_space=pl.ANY),
            const((1, D_MODEL)), const((1, D_MODEL)), const((D_MODEL, ne)), const((1, ne)),
        ],
        out_specs=[
            pl.BlockSpec((tm, D_MODEL), lambda i: (i, 0)),
            pl.BlockSpec((tm, LANES), lambda i: (i, 0)),
            pl.BlockSpec((tm, LANES), lambda i: (i, 0)),
            const((1, ne)),
        ],
        out_shape=[
            jax.ShapeDtypeStruct((t, D_MODEL), F32),
            jax.ShapeDtypeStruct((t, LANES), jnp.int32),
            jax.ShapeDtypeStruct((t, LANES), F32),
            jax.ShapeDtypeStruct((1, ne), F32),
        ],
        scratch_shapes=[pltpu.VMEM((D_MODEL, D_MODEL), BF16),
                        pltpu.VMEM((512, D_MODEL), F32),
                        pltpu.SemaphoreType.DMA(())],
        compiler_params=_params(("arbitrary",)),
        name="outproj_router",
    )(xp, xs, oap, oas, ogp, ogs, w_out, g1, b1, w_router, rbias)


FLAG_NEW_EXPERT = 1
FLAG_HAS_NEXT_EXPERT = 2
FLAG_WEIGHT_SLOT = 4


def _moe_kernel(bw0, bn, wexp, wlo, whi, wflag, wnext,
                src_hbm, dst_hbm, x_hbm, wg_hbm, wu_hbm, wd_hbm, y_hbm,
                xbuf, ybuf, xbb, wgf, wuf, wdf, wgb, wub, wdb, srcs, dsts, gsem, ssem, isem, wsem):
    nblk = src_hbm.shape[0]
    ring, blk = xbuf.shape[:2]
    ff = wgb.shape[1]

    def slot_of(j):
        return j % ring if isinstance(j, int) else lax.rem(j, ring)

    def src_dma(j, q):
        j = min(j, nblk - 1) if isinstance(j, int) else jnp.minimum(j, nblk - 1)
        return pltpu.make_async_copy(src_hbm.at[j], srcs.at[q], isem.at[0, q])

    def dst_dma(j, q):
        return pltpu.make_async_copy(dst_hbm.at[j], dsts.at[q], isem.at[1, q])

    def gather_copy(tok, r, q):
        return pltpu.make_async_copy(x_hbm.at[pl.ds(tok, 1), :], xbuf.at[q, pl.ds(r, 1), :], gsem.at[q])

    def scatter_copy(dst, r, q):
        return pltpu.make_async_copy(ybuf.at[q, pl.ds(r, 1), :], y_hbm.at[pl.ds(dst, 1), :], ssem.at[q])

    def issue_gather(q, rows=None):
        for r in (range(blk) if rows is None else rows):
            gather_copy(srcs[q, r], r, q).start()

    def wait_gather(q):
        for r in range(blk):
            gather_copy(0, r, q).wait()

    def issue_scatter(q, rows=None):
        for r in (range(blk) if rows is None else rows):
            scatter_copy(dsts[q, r], r, q).start()

    def wait_scatter(q):
        for r in range(blk):
            scatter_copy(0, r, q).wait()

    def weight_copies(e, ws):
        return (pltpu.make_async_copy(wg_hbm.at[e], wgf.at[ws], wsem.at[ws, 0]),
                pltpu.make_async_copy(wu_hbm.at[e], wuf.at[ws], wsem.at[ws, 1]),
                pltpu.make_async_copy(wd_hbm.at[e], wdf.at[ws], wsem.at[ws, 2]))

    def start_weights(e, ws):
        for cp in weight_copies(e, ws):
            cp.start(priority=1)

    def cast_weights(ws):
        step = 64
        for c in range(0, wgb.shape[0], step):
            wgb[pl.ds(c, step), :] = wgf[ws, pl.ds(c, step), :].astype(BF16)
            wub[pl.ds(c, step), :] = wuf[ws, pl.ds(c, step), :].astype(BF16)
        step = 16
        for c in range(0, wdb.shape[0], step):
            wdb[pl.ds(c, step), :] = wdf[ws, pl.ds(c, step), :].astype(BF16)

    def item(w, q, first, hooks=None):
        fl = wflag[w]
        anchor = hooks is not None
        hooks = hooks or (lambda: None,) * 4

        @pl.when((fl & FLAG_NEW_EXPERT) != 0)
        def _():
            ws = jnp.where((fl & FLAG_WEIGHT_SLOT) != 0, 1, 0)
            for cp in weight_copies(0, ws):
                cp.wait()
            cast_weights(ws)

            @pl.when((fl & FLAG_HAS_NEXT_EXPERT) != 0)
            def _():
                start_weights(wnext[w], 1 - ws)

        def after_ring_dmas():
            probe = pltpu.bitcast(xbuf[q, pl.ds(0, 8), pl.ds(0, LANES)], jnp.uint32)
            zero = lax.shift_right_logical(lax.shift_right_logical(probe, jnp.uint32(16)), jnp.uint32(16))
            return pltpu.bitcast(zero, F32)[0:1, 0:1]

        hooks[0]()
        gate = _dot(xbb[...], wgb[...]) + (after_ring_dmas() if anchor else 0.0)
        hooks[1]()
        up = _dot(xbb[...], wub[...]) + (after_ring_dmas() if anchor else 0.0)
        hooks[2]()
        hact = _silu(gate) * up
        rows = lax.broadcasted_iota(jnp.int32, (blk, ff), 0)
        hact = jnp.where((rows >= wlo[w]) & (rows < whi[w]), hact, 0.0)
        y = _dot(hact.astype(BF16), wdb[...])
        hooks[3]()
        if first:
            ybuf[q] = y
        else:
            ybuf[q] += y

    def step(b):
        static = isinstance(b, int)
        q = slot_of(b)
        wait_gather(q)
        if static:
            if b >= ring:
                wait_scatter(q)
        else:
            @pl.when(b >= ring)
            def _():
                wait_scatter(q)
        if not (static and b == 0):
            dst_dma(b - 1, slot_of(b + 2)).wait()
        src_dma(b + 2, slot_of(b + 2)).wait()
        src_dma(b + 3, q).start()
        dst_dma(b, q).start()
        xbb[...] = xbuf[q].astype(BF16)
        w0 = bw0[b]

        qn = slot_of(b + 2)
        half = blk // 2
        hooks = [lambda: issue_gather(qn, range(0, half)), lambda: issue_gather(qn, range(half, blk))]
        if static and b == 0:
            hooks += [lambda: None, lambda: None]
        else:
            hooks += [lambda: issue_scatter(qn, range(0, half)), lambda: issue_scatter(qn, range(half, blk))]
        item(w0, q, True, hooks)

        def extra(k, carry):
            item(w0 + k, q, False)
            return carry
        lax.fori_loop(1, bn[b], extra, 0)

    assert nblk >= 2 and ring == 3
    for j in range(3):
        src_dma(j, j).start()
    start_weights(wexp[0], 0)
    for j in range(2):
        src_dma(j, j).wait()
        issue_gather(j)

    step(0)
    lax.fori_loop(1, nblk, lambda b, c: (step(b), c)[1], 0)

    last = nblk - 1
    dst_dma(last, last % ring).wait()
    issue_scatter(last % ring)
    src_dma(last + 3, last % ring).wait()
    wait_gather((last + 1) % ring)
    wait_gather((last + 2) % ring)
    for j in range(max(0, nblk - ring), nblk):
        wait_scatter(j % ring)


def _moe_experts(x1, plan, w_gate, w_up, w_down, n_out_rows):
    src, dst = plan[-2:]
    ne, d, ff = w_gate.shape
    anyspec = pl.BlockSpec(memory_space=pl.ANY)
    grid_spec = pltpu.PrefetchScalarGridSpec(
        num_scalar_prefetch=7,
        grid=(1,),
        in_specs=[anyspec] * 6,
        out_specs=anyspec,
        scratch_shapes=[
            pltpu.VMEM((MOE_RING, MOE_BLK, d), F32),
            pltpu.VMEM((MOE_RING, MOE_BLK, d), F32),
            pltpu.VMEM((MOE_BLK, d), BF16),
            pltpu.VMEM((2, d, ff), F32),
            pltpu.VMEM((2, d, ff), F32),
            pltpu.VMEM((2, ff, d), F32),
            pltpu.VMEM((d, ff), BF16),
            pltpu.VMEM((d, ff), BF16),
            pltpu.VMEM((ff, d), BF16),
            pltpu.SMEM((MOE_RING, MOE_BLK), jnp.int32),
            pltpu.SMEM((MOE_RING, MOE_BLK), jnp.int32),
            pltpu.SemaphoreType.DMA((MOE_RING,)),
            pltpu.SemaphoreType.DMA((MOE_RING,)),
            pltpu.SemaphoreType.DMA((2, MOE_RING)),
            pltpu.SemaphoreType.DMA((2, 3)),
        ],
    )
    return pl.pallas_call(
        _moe_kernel,
        grid_spec=grid_spec,
        out_shape=jax.ShapeDtypeStruct((n_out_rows, d), F32),
        compiler_params=_params(("arbitrary",)),
        name="moe_experts",
    )(*plan[:-2], src, dst, x1, w_gate, w_up, w_down)


def _dispatch_plan(eidx, counts_f):
    t = eidx.shape[0]
    ne = counts_f.shape[-1]
    n = t * TOP_K
    blk = MOE_BLK
    assert n % blk == 0
    nblk = n // blk
    nwmax = nblk + ne - 1
    bits = max(n - 1, 1).bit_length()
    flat = jnp.arange(n, dtype=jnp.int32)
    skeys = jnp.sort(eidx.reshape(-1) * (1 << bits) + flat)
    order = skeys & ((1 << bits) - 1)
    tok = order // TOP_K
    src = tok.reshape(nblk, blk)
    dst = ((order % TOP_K) * t + tok).reshape(nblk, blk)

    counts = counts_f.reshape(ne).astype(jnp.int32)
    ends = jnp.cumsum(counts)
    starts = ends - counts
    fb = starts // blk
    nit = jnp.where(counts > 0, (ends + blk - 1) // blk - fb, 0)
    icum = jnp.cumsum(nit)
    nwork = icum[-1]
    eids = jnp.arange(ne, dtype=jnp.int32)
    nxt = jnp.flip(lax.cummin(jnp.flip(jnp.where(counts > 0, eids, ne)), axis=0))
    nxt_after = jnp.concatenate([nxt[1:], jnp.full((1,), ne, jnp.int32)])
    used_rank = jnp.cumsum((counts > 0).astype(jnp.int32)) - 1

    w = jnp.arange(nwmax, dtype=jnp.int32)
    onehot = (w[:, None] >= (icum - nit)[None, :]) & (w[:, None] < icum[None, :])
    pick = lambda v: jnp.sum(jnp.where(onehot, v[None, :], 0), axis=1)
    we = pick(eids)
    wb = pick(fb) + w - pick(icum - nit)
    lo = jnp.clip(pick(starts) - wb * blk, 0, blk)
    hi = jnp.clip(pick(ends) - wb * blk, 0, blk)
    wn = pick(nxt_after)
    wb = jnp.where(w < nwork, wb, nblk)
    prev_e = jnp.concatenate([jnp.full((1,), -1, jnp.int32), we[:-1]])
    flags = (jnp.where(we != prev_e, FLAG_NEW_EXPERT, 0)
             + jnp.where(wn < ne, FLAG_HAS_NEXT_EXPERT, 0)
             + (pick(used_rank) % 2) * FLAG_WEIGHT_SLOT)
    wn = jnp.minimum(wn, ne - 1)
    in_block = wb[None, :] == jnp.arange(nblk, dtype=jnp.int32)[:, None]
    bn = jnp.sum(in_block.astype(jnp.int32), axis=1)
    bw0 = jnp.cumsum(bn) - bn
    i32 = lambda v: v.astype(jnp.int32)
    return (i32(bw0), i32(bn), i32(we), i32(lo), i32(hi), i32(flags), i32(wn), src, dst)


def _final_kernel(ntp, *refs):
    x1_ref, gate_ref = refs[0], refs[1]
    y_refs = refs[2:2 + TOP_K]
    (wsg_hbm, wsu_hbm, wsd_hbm, g2_ref, b2_ref, outp_ref, outs_ref,
     wsg, wsu, wsd, stage_a, stage_b, sem) = refs[2 + TOP_K:]

    @pl.when(pl.program_id(0) == 0)
    def _():
        _load_bf16(wsg_hbm, wsg, stage_a, sem, stage_a.shape[0])
        _load_bf16(wsu_hbm, wsu, stage_a, sem, stage_a.shape[0])
        _load_bf16(wsd_hbm, wsd, stage_b, sem, stage_b.shape[0])

    x1 = x1_ref[...]
    xb = x1.astype(BF16)
    hact = _silu(_dot(xb, wsg[...])) * _dot(xb, wsu[...])
    f = _dot(hact.astype(BF16), wsd[...])
    gates = gate_ref[...]
    routed = jnp.zeros_like(x1)
    for kk in range(TOP_K):
        routed = routed + y_refs[kk][...] * gates[:, kk:kk + 1]
    out = _layernorm(DEEPNORM_ALPHA * x1 + (f + routed), g2_ref[...], b2_ref[...])

    @pl.when(pl.program_id(0) < ntp)
    def _():
        outp_ref[...] = out

    @pl.when(pl.program_id(0) >= ntp)
    def _():
        outs_ref[...] = out


def _final(x1, gates, ycomb, w_sg, w_su, w_sd, g2, b2, t_prompt):
    t = x1.shape[0]
    tm = ROW_TILE
    assert t % tm == 0 and t_prompt % tm == 0
    nt = t // tm
    ntp = t_prompt // tm
    ff = w_sg.shape[1]
    const = lambda shape: pl.BlockSpec(shape, lambda i: tuple(0 for _ in shape))
    yspec = lambda kk: pl.BlockSpec((tm, D_MODEL), lambda i: (kk * nt + i, 0))
    anyspec = pl.BlockSpec(memory_space=pl.ANY)
    return pl.pallas_call(
        functools.partial(_final_kernel, ntp),
        grid=(nt,),
        in_specs=[pl.BlockSpec((tm, D_MODEL), lambda i: (i, 0)),
                  pl.BlockSpec((tm, LANES), lambda i: (i, 0))]
                 + [yspec(kk) for kk in range(TOP_K)]
                 + [anyspec, anyspec, anyspec, const((1, D_MODEL)), const((1, D_MODEL))],
        out_specs=[pl.BlockSpec((tm, D_MODEL), lambda i: (jnp.minimum(i, ntp - 1), 0)),
                   pl.BlockSpec((tm, D_MODEL), lambda i: (jnp.maximum(i - ntp, 0), 0))],
        out_shape=[jax.ShapeDtypeStruct((t_prompt, D_MODEL), F32),
                   jax.ShapeDtypeStruct((t - t_prompt, D_MODEL), F32)],
        scratch_shapes=[pltpu.VMEM((D_MODEL, ff), BF16),
                        pltpu.VMEM((D_MODEL, ff), BF16),
                        pltpu.VMEM((ff, D_MODEL), BF16),
                        pltpu.VMEM((512, ff), F32),
                        pltpu.VMEM((128, D_MODEL), F32),
                        pltpu.SemaphoreType.DMA(())],
        compiler_params=_params(("arbitrary",)),
        name="shared_combine_ln",
    )(x1, gates, *([ycomb] * TOP_K), w_sg, w_su, w_sd, g2, b2)


def kernel(x_prompt, x_sample, cache_k_win, cache_v_win, state_gla, w_in, attn_sinks, attn_norm_g, w_gla_a2,
           b_gla_a, gla_norm_g, w_out, ln1_g, ln1_b, w_router, router_bias, w_exp_gate, w_exp_up, w_exp_down,
           w_sh_gate, w_sh_up, w_sh_down, ln2_g, ln2_b):
    assert w_in.shape[0] == DEPTH == 1 and x_prompt.shape[0] == 1 and x_sample.shape[1] == 1
    t = x_prompt.shape[1]
    bsz = x_sample.shape[0]
    win = cache_k_win.shape[2]
    assert t % WINDOW == 0 and win == WINDOW

    wag = jnp.pad(w_in[0][:, MAIN_COLS:], ((0, 0), (0, LANES - GLA_RANK)))
    w_in0 = w_in[0].astype(BF16)
    wa2 = jnp.pad(w_gla_a2[0], ((0, LANES - GLA_RANK), (0, 0)))
    ba = b_gla_a[0].reshape(1, GLA_QK_COLS)
    sinks = jnp.pad(attn_sinks[0].reshape(1, N_HEADS), ((0, 0), (0, LANES - N_HEADS)))
    ag = attn_norm_g[0].reshape(1, Q_COLS)
    gn = gla_norm_g[0].reshape(1, GLA_WIDTH)

    xp = x_prompt[0]
    hp, lap = _inproj(xp, w_in0, wag, wa2, ba)
    cp, sap, sbp = _rope_tables(jnp.arange(t, dtype=jnp.int32))
    oa_p, nk_p, nv_p = _swa_prompt(hp, cp, sap, sbp, attn_sinks[0], ag)
    og_p, s_p = _gla_prompt(hp, lap, gn)

    xs = x_sample[:, 0]
    hs, las = _inproj(xs, w_in0, wag, wa2, ba)
    cs, sas, sbs = _rope_tables(PAST_LEN + jnp.arange(1, dtype=jnp.int32))
    lane_head = jnp.arange(Q_COLS, dtype=jnp.int32)[:, None] // HEAD_DIM
    emat = (lane_head == jnp.arange(LANES, dtype=jnp.int32)[None, :]).astype(BF16)
    hs3 = hs.reshape(bsz, 1, MAIN_COLS)
    oa_s, nk_s, nv_s = _swa_decode(hs3, cache_k_win[0].reshape(bsz, win, KV_COLS),
                                   cache_v_win[0].reshape(bsz, win, KV_COLS), cs, sas, sbs, sinks, ag,
                                   emat, emat.T)
    og_s, s_s = _gla_decode(hs3, las.reshape(bsz, 1, GLA_QK_COLS), state_gla[0], gn)

    t_all = t + bsz
    x1, eidx, gates, counts = _outproj_router(
        xp, xs, oa_p, oa_s.reshape(bsz, Q_COLS), og_p, og_s.reshape(bsz, GLA_WIDTH), w_out[0],
        ln1_g[0].reshape(1, D_MODEL), ln1_b[0].reshape(1, D_MODEL), w_router[0], router_bias[0].reshape(1, -1))
    plan = _dispatch_plan(eidx[:, :TOP_K], counts)
    ycomb = _moe_experts(x1, plan, w_exp_gate[0], w_exp_up[0], w_exp_down[0], TOP_K * t_all)
    y_p, y_s = _final(x1, gates, ycomb, w_sh_gate[0], w_sh_up[0], w_sh_down[0],
                      ln2_g[0].reshape(1, D_MODEL), ln2_b[0].reshape(1, D_MODEL), t)

    y_prompt = y_p.reshape(1, t, D_MODEL)
    y_sample = y_s.reshape(bsz, 1, D_MODEL)
    new_k_prompt = nk_p.reshape(1, 1, win, N_KV_HEADS, HEAD_DIM)
    new_v_prompt = nv_p.reshape(1, 1, win, N_KV_HEADS, HEAD_DIM)
    new_s_prompt = s_p.reshape(1, 1, GLA_HEADS, GLA_DK, GLA_DV)
    new_k_sample = nk_s.reshape(1, bsz, win, N_KV_HEADS, HEAD_DIM)
    new_v_sample = nv_s.reshape(1, bsz, win, N_KV_HEADS, HEAD_DIM)
    new_s_sample = s_s.reshape(1, bsz, GLA_HEADS, GLA_DK, GLA_DV)
    return (y_prompt, y_sample, new_k_prompt, new_v_prompt, new_s_prompt,
            new_k_sample, new_v_sample, new_s_sample)
```

```python
import functools

import jax
import jax.numpy as jnp
from jax import lax
from jax.experimental import pallas as pl
from jax.experimental.pallas import tpu as pltpu

F32 = jnp.float32
BF16 = jnp.bfloat16

D_MODEL = 2048
PAST_LEN = 16384
HEAD_DIM = 64
N_HEADS = 16
N_KV_HEADS = 4
WINDOW = 128
ROPE_DIM = 16
ROPE_THETA = 500000.0
GLA_HEADS = 4
GLA_DV = 256
GLA_DK = 128
GLA_RANK = 16
GLA_TAU = 16.0
Q_COLS = N_HEADS * HEAD_DIM
KV_COLS = N_KV_HEADS * HEAD_DIM
GLA_QK_COLS = GLA_HEADS * GLA_DK
GLA_WIDTH = GLA_HEADS * GLA_DV
MAIN_COLS = Q_COLS + 2 * KV_COLS + 2 * GLA_QK_COLS + 2 * GLA_WIDTH
TOP_K = 8
N_GROUPS = 8
TOPK_GROUPS = 4
ROUTED_SCALE = 2.5
DEPTH = 1
DEEPNORM_ALPHA = (2 * DEPTH) ** 0.25
LN_EPS = 1e-5
RMS_EPS = 1e-6

LANES = 128
COL_BLOCK = 512
GLA_CHUNK = 64
GLA_SUB = 16
EXP_CLAMP = 80.0
ROW_TILE = 128
MOE_BLK = 128
MOE_RING = 3
VMEM_LIMIT = 56 * 1024 * 1024


def _pick(n, cands):
    for c in cands:
        if n % c == 0:
            return c
    raise ValueError(f"no tile in {cands} divides {n}")


def _params(sem):
    return pltpu.CompilerParams(dimension_semantics=sem, vmem_limit_bytes=VMEM_LIMIT)


def _dot(a, b):
    return jnp.dot(a, b, preferred_element_type=F32)


def _dot_nt(a, b):
    return lax.dot_general(a, b, (((1,), (1,)), ((), ())), preferred_element_type=F32)


def _dot_tn(a, b):
    return lax.dot_general(a, b, (((0,), (0,)), ((), ())), preferred_element_type=F32)


def _split3(x):
    hi = x.astype(BF16)
    r1 = x - hi.astype(F32)
    mid = r1.astype(BF16)
    lo = (r1 - mid.astype(F32)).astype(BF16)
    return hi, mid, lo


def _pack_bf16_pair(a, b):
    hi = pltpu.bitcast(a.astype(BF16).astype(F32), jnp.uint32)
    lo = pltpu.bitcast(b.astype(BF16).astype(F32), jnp.uint32)
    return hi | lax.shift_right_logical(lo, jnp.uint32(16))


def _unpack_bf16_pair(w):
    hi = pltpu.bitcast(w & jnp.uint32(0xFFFF0000), F32)
    lo = pltpu.bitcast(lax.shift_left(w, jnp.uint32(16)), F32)
    return hi, lo


def _tile_lanes(t, width):
    reps = width // t.shape[-1]
    return t if reps == 1 else jnp.concatenate([t] * reps, axis=-1)


def _rope(x, c, sa, sb):
    w = x.shape[-1]
    return (x * _tile_lanes(c, w) + pltpu.roll(x, 8, 1) * _tile_lanes(sa, w)
            + pltpu.roll(x, w - 8, 1) * _tile_lanes(sb, w))


def _rope_tables(pos):
    half = ROPE_DIM // 2
    inv = 1.0 / (ROPE_THETA ** (jnp.arange(0, ROPE_DIM, 2, dtype=F32) / ROPE_DIM))
    ang = pos.astype(F32)[:, None] * inv[None, :]
    cos, sin = jnp.cos(ang), jnp.sin(ang)
    n = pos.shape[0]
    rest = HEAD_DIM - ROPE_DIM
    c = jnp.concatenate([cos, cos, jnp.ones((n, rest), F32)], axis=1)
    sa = jnp.concatenate([jnp.zeros((n, half), F32), sin, jnp.zeros((n, rest), F32)], axis=1)
    sb = jnp.concatenate([-sin, jnp.zeros((n, half + rest), F32)], axis=1)
    rep = LANES // HEAD_DIM
    return jnp.tile(c, (1, rep)), jnp.tile(sa, (1, rep)), jnp.tile(sb, (1, rep))


def _inproj_kernel(x_ref, w_ref, wag_ref, wa2_ref, ba_ref, h_ref, la_ref, xb_ref):
    @pl.when(pl.program_id(1) == 0)
    def _():
        xb = x_ref[...].astype(BF16)
        xb_ref[...] = xb
        ag = _dot(xb, wag_ref[...].astype(BF16))
        z = _dot(ag.astype(BF16), wa2_ref[...].astype(BF16)) + ba_ref[...]
        la_ref[...] = (jnp.minimum(z, 0.0) - jnp.log1p(jnp.exp(-jnp.abs(z)))) * (1.0 / GLA_TAU)

    h_ref[...] = _dot(xb_ref[...], w_ref[...])


def _inproj(x, w_in, wag, wa2, ba):
    t = x.shape[0]
    tm = _pick(t, (1024, 512, 256, 128))
    ncol = MAIN_COLS // COL_BLOCK
    return pl.pallas_call(
        _inproj_kernel,
        grid=(t // tm, ncol),
        in_specs=[
            pl.BlockSpec((tm, D_MODEL), lambda i, j: (i, 0)),
            pl.BlockSpec((D_MODEL, COL_BLOCK), lambda i, j: (0, j)),
            pl.BlockSpec((D_MODEL, LANES), lambda i, j: (0, 0)),
            pl.BlockSpec((LANES, GLA_QK_COLS), lambda i, j: (0, 0)),
            pl.BlockSpec((1, GLA_QK_COLS), lambda i, j: (0, 0)),
        ],
        out_specs=[
            pl.BlockSpec((tm, COL_BLOCK), lambda i, j: (i, j)),
            pl.BlockSpec((tm, GLA_QK_COLS), lambda i, j: (i, 0)),
        ],
        out_shape=[
            jax.ShapeDtypeStruct((t, MAIN_COLS), F32),
            jax.ShapeDtypeStruct((t, GLA_QK_COLS), F32),
        ],
        scratch_shapes=[pltpu.VMEM((tm, D_MODEL), BF16)],
        compiler_params=_params(("arbitrary", "arbitrary")),
        name="inproj",
    )(x, w_in, wag, wa2, ba)


def _half_variants(x):
    lane = lax.broadcasted_iota(jnp.int32, (x.shape[0], LANES), 1)
    lo = lane < HEAD_DIM
    out = []
    for g in range(N_KV_HEADS):
        chunk = x[:, (g // 2) * LANES:(g // 2 + 1) * LANES]
        swapped = pltpu.roll(chunk, HEAD_DIM, 1)
        u = g % 2
        for b in range(2):
            src = chunk if b == u else swapped
            keep = lo if b == 0 else jnp.logical_not(lo)
            out.append(jnp.where(keep, src, 0.0).astype(BF16))
    return out


def _swa_kernel(q_ref, kv_ref, c_ref, sa_ref, sb_ref, sink_ref, g_ref,
                oa_ref, nk_ref, nv_ref, kp_ref, vp_ref):
    i = pl.program_id(0)
    blk = q_ref.shape[0]
    c, sa, sb = c_ref[...], sa_ref[...], sb_ref[...]

    @pl.when(i == 0)
    def _():
        kp_ref[...] = jnp.zeros_like(kp_ref)
        vp_ref[...] = jnp.zeros_like(vp_ref)

    q = (_rope(q_ref[...], c, sa, sb) * (HEAD_DIM ** -0.5)).astype(BF16)
    kv = kv_ref[...]
    k = _rope(kv[:, :KV_COLS], c, sa, sb)
    v = kv[:, KV_COLS:]
    kc = _half_variants(k)
    vc = _half_variants(v)

    row = lax.broadcasted_iota(jnp.int32, (blk, 2 * blk), 0)
    col = lax.broadcasted_iota(jnp.int32, (blk, 2 * blk), 1)
    first_key = jnp.where(i > 0, 0, blk)
    mask = (col > row) & (col <= row + WINDOW) & (col >= first_key)

    pairs = []
    for p in range(N_HEADS // 2):
        qp = q[:, p * LANES:(p + 1) * LANES]
        acc = jnp.zeros((blk, LANES), F32)
        for b in range(2):
            h = 2 * p + b
            g = h // (N_HEADS // N_KV_HEADS)
            kcat = jnp.concatenate([kp_ref[2 * g + b], kc[2 * g + b]], axis=0)
            vcat = jnp.concatenate([vp_ref[2 * g + b], vc[2 * g + b]], axis=0)
            s = jnp.where(mask, _dot_nt(qp, kcat), -jnp.inf)
            sk = sink_ref[h]
            m = jnp.maximum(jnp.max(s, axis=1, keepdims=True), sk)
            e = jnp.exp(s - m)
            den = jnp.sum(e, axis=1, keepdims=True) + jnp.exp(sk - m)
            acc = acc + _dot((e / den).astype(BF16), vcat)
        pairs.append(acc)
    o = jnp.concatenate(pairs, axis=1)
    ms = jnp.mean(o * o, axis=1, keepdims=True)
    oa_ref[...] = (o * lax.rsqrt(ms + RMS_EPS) * g_ref[...]).astype(oa_ref.dtype)

    for n in range(2 * N_KV_HEADS):
        kp_ref[n] = kc[n]
        vp_ref[n] = vc[n]

    @pl.when(i == pl.num_programs(0) - 1)
    def _():
        nk_ref[...] = k
        nv_ref[...] = v


def _swa_prompt(h, c, sa, sb, sinks, g):
    t = h.shape[0]
    blk = WINDOW
    nb = t // blk
    return pl.pallas_call(
        _swa_kernel,
        grid=(nb,),
        in_specs=[
            pl.BlockSpec((blk, Q_COLS), lambda i: (i, 0)),
            pl.BlockSpec((blk, 2 * KV_COLS), lambda i: (i, Q_COLS // (2 * KV_COLS))),
            pl.BlockSpec((blk, LANES), lambda i: (i, 0)),
            pl.BlockSpec((blk, LANES), lambda i: (i, 0)),
            pl.BlockSpec((blk, LANES), lambda i: (i, 0)),
            pl.BlockSpec(memory_space=pltpu.SMEM),
            pl.BlockSpec((1, Q_COLS), lambda i: (0, 0)),
        ],
        out_specs=[
            pl.BlockSpec((blk, Q_COLS), lambda i: (i, 0)),
            pl.BlockSpec((blk, KV_COLS), lambda i: (0, 0)),
            pl.BlockSpec((blk, KV_COLS), lambda i: (0, 0)),
        ],
        out_shape=[
            jax.ShapeDtypeStruct((t, Q_COLS), BF16),
            jax.ShapeDtypeStruct((blk, KV_COLS), F32),
            jax.ShapeDtypeStruct((blk, KV_COLS), F32),
        ],
        scratch_shapes=[pltpu.VMEM((2 * N_KV_HEADS, blk, LANES), BF16),
                        pltpu.VMEM((2 * N_KV_HEADS, blk, LANES), BF16)],
        compiler_params=_params(("arbitrary",)),
        name="swa_prompt",
    )(h, h, c, sa, sb, sinks, g)


def _silu(x):
    return x * jax.nn.sigmoid(x)


def _gla_kernel(qg_ref, kg_ref, v1_ref, v2_ref, r1_ref, r2_ref, la_ref, gn_ref,
                og_ref, sfin_ref, s_ref):
    i = pl.program_id(0)
    rows = qg_ref.shape[0]
    ch = GLA_CHUNK
    nsub = ch // GLA_SUB

    @pl.when(i == 0)
    def _():
        s_ref[...] = jnp.zeros_like(s_ref)

    rr = lax.broadcasted_iota(jnp.int32, (ch, ch), 0)
    cc = lax.broadcasted_iota(jnp.int32, (ch, ch), 1)
    tri = jnp.where(rr >= cc, 1.0, 0.0).astype(BF16)
    pre = jnp.where(cc < (rr // GLA_SUB) * GLA_SUB, 1.0, 0.0).astype(BF16)
    mr = lax.broadcasted_iota(jnp.int32, (ch, nsub * ch), 0)
    mc = lax.broadcasted_iota(jnp.int32, (ch, nsub * ch), 1)
    amask = ((mc // ch) == (mr // GLA_SUB)) & ((mc % ch) <= mr)
    ones_t = jnp.ones((ch, LANES), BF16)

    def chunk(ci, carry):
        r0 = pl.multiple_of(ci * ch, ch)
        for hh in range(GLA_HEADS):
            dk = slice(hh * GLA_DK, (hh + 1) * GLA_DK)
            vref, rref = (v1_ref, r1_ref) if hh < 2 else (v2_ref, r2_ref)
            dv = slice((hh % 2) * GLA_DV, (hh % 2 + 1) * GLA_DV)
            q = qg_ref[pl.ds(r0, ch), dk] * (GLA_DK ** -0.5)
            k = kg_ref[pl.ds(r0, ch), dk]
            la = la_ref[pl.ds(r0, ch), dk]
            v = vref[pl.ds(r0, ch), dv]
            r = rref[pl.ds(r0, ch), dv]
            hi, mid, lo = _split3(la)
            b = _dot(tri, hi) + _dot(tri, mid) + _dot(tri, lo)
            bs = _dot(pre, hi) + _dot(pre, mid) + _dot(pre, lo)
            blast = b[ch - 1:ch, :]
            bl_col = _dot_tn(hi, ones_t) + _dot_tn(mid, ones_t) + _dot_tn(lo, ones_t)
            dec_col = jnp.exp(jnp.concatenate([bl_col, bl_col], axis=1))

            qt = (q * jnp.exp(b - bs)).astype(BF16)
            kparts = []
            for si in range(nsub):
                ref_row = bs[si * GLA_SUB:si * GLA_SUB + 1, :]
                kparts.append((k * jnp.exp(jnp.minimum(ref_row - b, EXP_CLAMP))).astype(BF16))
            kcat = jnp.concatenate(kparts, axis=0)
            a = jnp.where(amask, _dot_nt(qt, kcat), 0.0).astype(BF16)
            vb = v.astype(BF16)
            vrep = jnp.concatenate([vb] * nsub, axis=0)
            s_old = s_ref[hh]
            o = _dot(a, vrep) + _dot((q * jnp.exp(b)).astype(BF16), s_old.astype(BF16))
            kd = (k * jnp.exp(blast - b)).astype(BF16)
            s_ref[hh] = dec_col * s_old + _dot_tn(kd, vb)

            ms = jnp.mean(o * o, axis=1, keepdims=True)
            gn = gn_ref[:, hh * GLA_DV:(hh + 1) * GLA_DV]
            y = o * lax.rsqrt(ms + RMS_EPS) * gn
            og_ref[pl.ds(r0, ch), hh * GLA_DV:(hh + 1) * GLA_DV] = (y * _silu(r)).astype(og_ref.dtype)
        return carry

    lax.fori_loop(0, rows // ch, chunk, 0)

    @pl.when(i == pl.num_programs(0) - 1)
    def _():
        sfin_ref[...] = s_ref[...]


def _gla_prompt(h, la, gn):
    t = h.shape[0]
    rows = _pick(t, (512, 256, 128, 64))
    cb = COL_BLOCK
    base = (Q_COLS + 2 * KV_COLS) // cb
    spec = lambda j: pl.BlockSpec((rows, cb), lambda i: (i, j))
    return pl.pallas_call(
        _gla_kernel,
        grid=(t // rows,),
        in_specs=[spec(base), spec(base + 1), spec(base + 2), spec(base + 3), spec(base + 4), spec(base + 5),
                  pl.BlockSpec((rows, GLA_QK_COLS), lambda i: (i, 0)),
                  pl.BlockSpec((1, GLA_WIDTH), lambda i: (0, 0))],
        out_specs=[
            pl.BlockSpec((rows, GLA_WIDTH), lambda i: (i, 0)),
            pl.BlockSpec((GLA_HEADS, GLA_DK, GLA_DV), lambda i: (0, 0, 0)),
        ],
        out_shape=[
            jax.ShapeDtypeStruct((t, GLA_WIDTH), BF16),
            jax.ShapeDtypeStruct((GLA_HEADS, GLA_DK, GLA_DV), F32),
        ],
        scratch_shapes=[pltpu.VMEM((GLA_HEADS, GLA_DK, GLA_DV), F32)],
        compiler_params=_params(("arbitrary",)),
        name="gla_prompt",
    )(h, h, h, h, h, h, la, gn)


def _dup_heads(x):
    lane = lax.broadcasted_iota(jnp.int32, (x.shape[0], LANES), 1)
    lo = lane < HEAD_DIM
    out = []
    for g in range(N_KV_HEADS):
        chunk = x[:, (g // 2) * LANES:(g // 2 + 1) * LANES]
        swapped = pltpu.roll(chunk, HEAD_DIM, 1)
        out.append(jnp.where(lo, chunk, swapped) if g % 2 == 0 else jnp.where(lo, swapped, chunk))
    return out


def _swa_dec_kernel(q_ref, kv_ref, ck_ref, cv_ref, c_ref, sa_ref, sb_ref, sink_ref, g_ref, e_ref, et_ref,
                    oa_ref, nk_ref, nv_ref):
    win = ck_ref.shape[1]
    c, sa, sb = c_ref[...], sa_ref[...], sb_ref[...]
    q8 = jnp.broadcast_to(q_ref[0], (8, Q_COLS))
    kv8 = jnp.broadcast_to(kv_ref[0], (8, 2 * KV_COLS))
    q = _rope(q8, c, sa, sb) * (HEAD_DIM ** -0.5)
    knew = _rope(kv8[:, :KV_COLS], c, sa, sb)
    vnew = kv8[:, KV_COLS:]

    row = lax.broadcasted_iota(jnp.int32, (win, KV_COLS), 0)
    last = row == win - 1
    keys = jnp.where(last, jnp.broadcast_to(knew[0:1], (win, KV_COLS)), pltpu.roll(ck_ref[0], win - 1, 0))
    vals = jnp.where(last, jnp.broadcast_to(vnew[0:1], (win, KV_COLS)), pltpu.roll(cv_ref[0], win - 1, 0))
    nk_ref[0] = keys
    nv_ref[0] = vals

    qb = q.astype(BF16).astype(F32)
    kd = _dup_heads(keys.astype(BF16).astype(F32))
    vd = _dup_heads(vals.astype(BF16).astype(F32))
    npair = N_HEADS // 2
    prod = jnp.concatenate(
        [kd[p // 2] * jnp.broadcast_to(qb[0:1, p * LANES:(p + 1) * LANES], (win, LANES)) for p in range(npair)],
        axis=1)
    st = _dot(prod.astype(BF16), e_ref[...])
    sk = sink_ref[...]
    m = jnp.maximum(jnp.max(st, axis=0, keepdims=True), sk)
    e = jnp.exp(st - m)
    den = jnp.sum(e, axis=0, keepdims=True) + jnp.exp(sk - m)
    pe = _dot((e / den).astype(BF16), et_ref[...])
    vcat = jnp.concatenate([vd[p // 2] for p in range(npair)], axis=1)
    o = jnp.sum(pe * vcat, axis=0, keepdims=True)
    ms = jnp.mean(o * o, axis=1, keepdims=True)
    oa_ref[0] = (o * lax.rsqrt(ms + RMS_EPS) * g_ref[...]).astype(oa_ref.dtype)


def _swa_decode(h3, ck, cv, c, sa, sb, sinks, g, e, et):
    bsz = h3.shape[0]
    win = ck.shape[1]
    const = lambda shape: pl.BlockSpec(shape, lambda b: tuple(0 for _ in shape))
    return pl.pallas_call(
        _swa_dec_kernel,
        grid=(bsz,),
        in_specs=[
            pl.BlockSpec((1, 1, Q_COLS), lambda b: (b, 0, 0)),
            pl.BlockSpec((1, 1, 2 * KV_COLS), lambda b: (b, 0, Q_COLS // (2 * KV_COLS))),
            pl.BlockSpec((1, win, KV_COLS), lambda b: (b, 0, 0)),
            pl.BlockSpec((1, win, KV_COLS), lambda b: (b, 0, 0)),
            const((1, LANES)), const((1, LANES)), const((1, LANES)), const((1, LANES)),
            const((1, Q_COLS)), const((Q_COLS, LANES)), const((LANES, Q_COLS)),
        ],
        out_specs=[
            pl.BlockSpec((1, 1, Q_COLS), lambda b: (b, 0, 0)),
            pl.BlockSpec((1, win, KV_COLS), lambda b: (b, 0, 0)),
            pl.BlockSpec((1, win, KV_COLS), lambda b: (b, 0, 0)),
        ],
        out_shape=[
            jax.ShapeDtypeStruct((bsz, 1, Q_COLS), BF16),
            jax.ShapeDtypeStruct((bsz, win, KV_COLS), F32),
            jax.ShapeDtypeStruct((bsz, win, KV_COLS), F32),
        ],
        compiler_params=_params(("arbitrary",)),
        name="swa_decode",
    )(h3, h3, ck, cv, c, sa, sb, sinks, g, e, et)


def _rows8(rows):
    w = rows[0].shape[1]
    ridx = lax.broadcasted_iota(jnp.int32, (8, w), 0)
    out = jnp.zeros((8, w), F32)
    for n, r in enumerate(rows):
        out = jnp.where(ridx == n, jnp.broadcast_to(r, (8, w)), out)
    return out


def _gla_dec_kernel(qg_ref, kg_ref, v1_ref, v2_ref, r1_ref, r2_ref, la_ref, s_ref, gn_ref, og_ref, sn_ref):
    ones8 = jnp.where(lax.broadcasted_iota(jnp.int32, (8, GLA_DV), 0) < 3, 1.0, 0.0).astype(BF16)
    outs = []
    for hh in range(GLA_HEADS):
        dk = slice(hh * GLA_DK, (hh + 1) * GLA_DK)
        vref, rref = (v1_ref, r1_ref) if hh < 2 else (v2_ref, r2_ref)
        dv = slice((hh % 2) * GLA_DV, (hh % 2 + 1) * GLA_DV)
        q = qg_ref[0][:, dk] * (GLA_DK ** -0.5)
        k = kg_ref[0][:, dk]
        la = la_ref[0][:, dk]
        v = vref[0][:, dv]
        r = rref[0][:, dv]
        s_old = s_ref[0, hh]
        dec = jnp.exp(la)
        qb = q.astype(BF16).astype(F32)
        kb = k.astype(BF16).astype(F32)
        score = jnp.sum(qb * kb, axis=1, keepdims=True)
        q8 = _rows8([q * dec]).astype(BF16)
        o = score * v + _dot(q8, s_old.astype(BF16))[0:1, :]
        dhi, dmid, dlo = _split3(dec)
        d8 = _rows8([dhi.astype(F32), dmid.astype(F32), dlo.astype(F32)]).astype(BF16)
        dec_col = _dot_tn(d8, ones8)
        k8 = _rows8([k]).astype(BF16)
        v8 = _rows8([v]).astype(BF16)
        sn_ref[0, hh] = dec_col * s_old + _dot_tn(k8, v8)
        ms = jnp.mean(o * o, axis=1, keepdims=True)
        gn = gn_ref[:, hh * GLA_DV:(hh + 1) * GLA_DV]
        outs.append(o * lax.rsqrt(ms + RMS_EPS) * gn * _silu(r))
    og_ref[0] = jnp.concatenate(outs, axis=1).astype(og_ref.dtype)


def _gla_decode(h3, la3, s0, gn):
    bsz = h3.shape[0]
    cb = COL_BLOCK
    base = (Q_COLS + 2 * KV_COLS) // cb
    spec = lambda j: pl.BlockSpec((1, 1, cb), lambda b: (b, 0, j))
    return pl.pallas_call(
        _gla_dec_kernel,
        grid=(bsz,),
        in_specs=[spec(base), spec(base + 1), spec(base + 2), spec(base + 3), spec(base + 4), spec(base + 5),
                  pl.BlockSpec((1, 1, GLA_QK_COLS), lambda b: (b, 0, 0)),
                  pl.BlockSpec((1, GLA_HEADS, GLA_DK, GLA_DV), lambda b: (b, 0, 0, 0)),
                  pl.BlockSpec((1, GLA_WIDTH), lambda b: (0, 0))],
        out_specs=[
            pl.BlockSpec((1, 1, GLA_WIDTH), lambda b: (b, 0, 0)),
            pl.BlockSpec((1, GLA_HEADS, GLA_DK, GLA_DV), lambda b: (b, 0, 0, 0)),
        ],
        out_shape=[
            jax.ShapeDtypeStruct((bsz, 1, GLA_WIDTH), BF16),
            jax.ShapeDtypeStruct(s0.shape, s0.dtype),
        ],
        compiler_params=_params(("arbitrary",)),
        name="gla_decode",
    )(h3, h3, h3, h3, h3, h3, la3, s0, gn)


def _layernorm(v, g, b):
    mu = jnp.mean(v, axis=1, keepdims=True)
    d = v - mu
    var = jnp.mean(d * d, axis=1, keepdims=True)
    return d * lax.rsqrt(var + LN_EPS) * g + b


def _load_bf16(w_hbm, dst_ref, stage_ref, sem, chunk):
    for s in range(w_hbm.shape[0] // chunk):
        cp = pltpu.make_async_copy(w_hbm.at[pl.ds(s * chunk, chunk), :], stage_ref, sem)
        cp.start()
        cp.wait()
        dst_ref[pl.ds(s * chunk, chunk), :] = stage_ref[...].astype(BF16)


def _route(logits, bias):
    rows, ne = logits.shape
    per = ne // N_GROUPS
    ninf = -jnp.inf
    lane = lax.broadcasted_iota(jnp.int32, (rows, ne), 1)
    gid = lane // per
    scores = jax.nn.sigmoid(logits)
    biased = scores + bias
    gfull = jnp.zeros((rows, ne), F32)
    gcols = []
    for g in range(N_GROUPS):
        ing = gid == g
        xg = jnp.where(ing, biased, ninf)
        m1 = jnp.max(xg, axis=1, keepdims=True)
        cnt = jnp.sum(jnp.where(xg == m1, 1.0, 0.0), axis=1, keepdims=True)
        m2 = jnp.max(jnp.where(xg < m1, xg, ninf), axis=1, keepdims=True)
        gs = m1 + jnp.where(cnt >= 2.0, m1, m2)
        gcols.append(gs)
        gfull = jnp.where(ing, gs, gfull)
    rank = jnp.zeros((rows, ne), jnp.int32)
    for g in range(N_GROUPS):
        beats = (gcols[g] > gfull) | ((gcols[g] == gfull) & (gid > g))
        rank = rank + jnp.where(beats, 1, 0)
    masked = jnp.where(rank < TOPK_GROUPS, biased, ninf)

    lane_f = lane.astype(F32)
    out_lane = lax.broadcasted_iota(jnp.int32, (rows, LANES), 1)
    eout = jnp.zeros((rows, LANES), F32)
    gout = jnp.zeros((rows, LANES), F32)
    total = jnp.zeros((rows, 1), F32)
    chosen = jnp.zeros((rows, ne), F32)
    for kk in range(TOP_K):
        m = jnp.max(masked, axis=1, keepdims=True)
        idx = jnp.min(jnp.where(masked == m, lane_f, float(ne)), axis=1, keepdims=True)
        hit = lane_f == idx
        gk = jnp.sum(jnp.where(hit, scores, 0.0), axis=1, keepdims=True)
        masked = jnp.where(hit, ninf, masked)
        chosen = jnp.where(hit, 1.0, chosen)
        total = total + gk
        eout = jnp.where(out_lane == kk, idx, eout)
        gout = jnp.where(out_lane == kk, gk, gout)
    counts = jnp.sum(chosen, axis=0, keepdims=True)
    return eout.astype(jnp.int32), gout / total * ROUTED_SCALE, counts


def _outproj_kernel(ntp, xp_ref, xs_ref, oap_ref, oas_ref, ogp_ref, ogs_ref, wout_hbm, g1_ref, b1_ref, wr_ref,
                    rb_ref, x1_ref, x1p_ref, eidx_ref, gate_ref, cnt_ref, wo_ref, stage_ref, sem):
    @pl.when(pl.program_id(0) == 0)
    def _():
        _load_bf16(wout_hbm, wo_ref, stage_ref, sem, stage_ref.shape[0])
        cnt_ref[...] = jnp.zeros_like(cnt_ref)

    is_prompt = pl.program_id(0) < ntp
    x = jnp.where(is_prompt, xp_ref[...], xs_ref[...])
    oa = jnp.where(is_prompt, oap_ref[...], oas_ref[...])
    og = jnp.where(is_prompt, ogp_ref[...], ogs_ref[...])
    half = oa.shape[1]
    mix = _dot(oa, wo_ref[pl.ds(0, half), :]) + _dot(og, wo_ref[pl.ds(half, half), :])
    x1 = _layernorm(DEEPNORM_ALPHA * x + mix, g1_ref[...], b1_ref[...])
    x1_ref[...] = x1
    x1p_ref[...] = _pack_bf16_pair(x1[:, :D_MODEL // 2], x1[:, D_MODEL // 2:])
    logits = _dot(x1.astype(BF16), wr_ref[...].astype(BF16))
    eidx, gates, counts = _route(logits, rb_ref[...])
    eidx_ref[...] = eidx
    gate_ref[...] = gates
    cnt_ref[...] += counts


def _outproj_router(xp, xs, oap, oas, ogp, ogs, w_out, g1, b1, w_router, rbias):
    tm = ROW_TILE
    tp, ts = xp.shape[0], xs.shape[0]
    assert tp % tm == 0 and ts % tm == 0
    ntp = tp // tm
    t = tp + ts
    ne = w_router.shape[1]
    const = lambda shape: pl.BlockSpec(shape, lambda i: tuple(0 for _ in shape))
    prompt = lambda width: pl.BlockSpec((tm, width), lambda i: (jnp.minimum(i, ntp - 1), 0))
    sample = lambda width: pl.BlockSpec((tm, width), lambda i: (jnp.maximum(i - ntp, 0), 0))
    return pl.pallas_call(
        functools.partial(_outproj_kernel, ntp),
        grid=(t // tm,),
        in_specs=[
            prompt(D_MODEL), sample(D_MODEL), prompt(Q_COLS), sample(Q_COLS), prompt(GLA_WIDTH), sample(GLA_WIDTH),
            pl.BlockSpec(memory_space=pl.ANY),
            const((1, D_MODEL)), const((1, D_MODEL)), const((D_MODEL, ne)), const((1, ne)),
        ],
        out_specs=[
            pl.BlockSpec((tm, D_MODEL), lambda i: (i, 0)),
            pl.BlockSpec((tm, D_MODEL // 2), lambda i: (i, 0)),
            pl.BlockSpec((tm, LANES), lambda i: (i, 0)),
            pl.BlockSpec((tm, LANES), lambda i: (i, 0)),
            const((1, ne)),
        ],
        out_shape=[
            jax.ShapeDtypeStruct((t, D_MODEL), F32),
            jax.ShapeDtypeStruct((t, D_MODEL // 2), jnp.uint32),
            jax.ShapeDtypeStruct((t, LANES), jnp.int32),
            jax.ShapeDtypeStruct((t, LANES), F32),
            jax.ShapeDtypeStruct((1, ne), F32),
        ],
        scratch_shapes=[pltpu.VMEM((D_MODEL, D_MODEL), BF16),
                        pltpu.VMEM((512, D_MODEL), F32),
                        pltpu.SemaphoreType.DMA(())],
        compiler_params=_params(("arbitrary",)),
        name="outproj_router",
    )(xp, xs, oap, oas, ogp, ogs, w_out, g1, b1, w_router, rbias)


FLAG_NEW_EXPERT = 1
FLAG_HAS_NEXT_EXPERT = 2
FLAG_WEIGHT_SLOT = 4


def _moe_kernel(bw0, bn, wexp, wlo, whi, wflag, wnext,
                src_hbm, dst_hbm, x_hbm, wg_hbm, wu_hbm, wd_hbm, y_hbm,
                xbuf, ybuf, xbb, wgf, wuf, wdf, wgb, wub, wdb, srcs, dsts, gsem, ssem, isem, wsem):
    nblk = src_hbm.shape[0]
    ring, blk = xbuf.shape[:2]
    ff = wgb.shape[1]

    def slot_of(j):
        return j % ring if isinstance(j, int) else lax.rem(j, ring)

    def src_dma(j, q):
        j = min(j, nblk - 1) if isinstance(j, int) else jnp.minimum(j, nblk - 1)
        return pltpu.make_async_copy(src_hbm.at[j], srcs.at[q], isem.at[0, q])

    def dst_dma(j, q):
        return pltpu.make_async_copy(dst_hbm.at[j], dsts.at[q], isem.at[1, q])

    def gather_copy(tok, r, q):
        return pltpu.make_async_copy(x_hbm.at[pl.ds(tok, 1), :], xbuf.at[q, pl.ds(r, 1), :], gsem.at[q])

    def scatter_copy(dst, r, q):
        return pltpu.make_async_copy(ybuf.at[q, pl.ds(r, 1), :], y_hbm.at[pl.ds(dst, 1), :], ssem.at[q])

    def issue_gather(q, rows=None):
        for r in (range(blk) if rows is None else rows):
            gather_copy(srcs[q, r], r, q).start()

    def wait_gather(q):
        for r in range(blk):
            gather_copy(0, r, q).wait()

    def issue_scatter(q, rows=None):
        for r in (range(blk) if rows is None else rows):
            scatter_copy(dsts[q, r], r, q).start()

    def wait_scatter(q):
        for r in range(blk):
            scatter_copy(0, r, q).wait()

    def weight_copies(e, ws):
        return (pltpu.make_async_copy(wg_hbm.at[e], wgf.at[ws], wsem.at[ws, 0]),
                pltpu.make_async_copy(wu_hbm.at[e], wuf.at[ws], wsem.at[ws, 1]),
                pltpu.make_async_copy(wd_hbm.at[e], wdf.at[ws], wsem.at[ws, 2]))

    def start_weights(e, ws):
        for cp in weight_copies(e, ws):
            cp.start(priority=1)

    def cast_weights(ws):
        step = 64
        for c in range(0, wgb.shape[0], step):
            wgb[pl.ds(c, step), :] = wgf[ws, pl.ds(c, step), :].astype(BF16)
            wub[pl.ds(c, step), :] = wuf[ws, pl.ds(c, step), :].astype(BF16)
        step = 16
        for c in range(0, wdb.shape[0], step):
            wdb[pl.ds(c, step), :] = wdf[ws, pl.ds(c, step), :].astype(BF16)

    def item(w, q, first, hooks=None):
        fl = wflag[w]
        anchor = hooks is not None
        hooks = hooks or (lambda: None,) * 4

        @pl.when((fl & FLAG_NEW_EXPERT) != 0)
        def _():
            ws = jnp.where((fl & FLAG_WEIGHT_SLOT) != 0, 1, 0)
            for cp in weight_copies(0, ws):
                cp.wait()
            cast_weights(ws)

            @pl.when((fl & FLAG_HAS_NEXT_EXPERT) != 0)
            def _():
                start_weights(wnext[w], 1 - ws)

        def after_ring_dmas():
            probe = xbuf[q, pl.ds(0, 8), pl.ds(0, LANES)]
            zero = lax.shift_right_logical(lax.shift_right_logical(probe, jnp.uint32(16)), jnp.uint32(16))
            return pltpu.bitcast(zero, F32)[0:1, 0:1]

        hooks[0]()
        gate = _dot(xbb[...], wgb[...]) + (after_ring_dmas() if anchor else 0.0)
        hooks[1]()
        up = _dot(xbb[...], wub[...]) + (after_ring_dmas() if anchor else 0.0)
        hooks[2]()
        hact = _silu(gate) * up
        rows = lax.broadcasted_iota(jnp.int32, (blk, ff), 0)
        hact = jnp.where((rows >= wlo[w]) & (rows < whi[w]), hact, 0.0)
        y = _dot(hact.astype(BF16), wdb[...])
        hooks[3]()
        half_d = y.shape[1] // 2
        if first:
            ybuf[q] = _pack_bf16_pair(y[:, :half_d], y[:, half_d:])
        else:
            hi, lo = _unpack_bf16_pair(ybuf[q])
            ybuf[q] = _pack_bf16_pair(hi + y[:, :half_d], lo + y[:, half_d:])

    def step(b):
        static = isinstance(b, int)
        q = slot_of(b)
        wait_gather(q)
        if static:
            if b >= ring:
                wait_scatter(q)
        else:
            @pl.when(b >= ring)
            def _():
                wait_scatter(q)
        if not (static and b == 0):
            dst_dma(b - 1, slot_of(b + 2)).wait()
        src_dma(b + 2, slot_of(b + 2)).wait()
        src_dma(b + 3, q).start()
        dst_dma(b, q).start()
        hi, lo = _unpack_bf16_pair(xbuf[q])
        half_d = xbb.shape[1] // 2
        xbb[:, pl.ds(0, half_d)] = hi.astype(BF16)
        xbb[:, pl.ds(half_d, half_d)] = lo.astype(BF16)
        w0 = bw0[b]

        qn = slot_of(b + 2)
        half = blk // 2
        hooks = [lambda: issue_gather(qn, range(0, half)), lambda: issue_gather(qn, range(half, blk))]
        if static and b == 0:
            hooks += [lambda: None, lambda: None]
        else:
            hooks += [lambda: issue_scatter(qn, range(0, half)), lambda: issue_scatter(qn, range(half, blk))]
        item(w0, q, True, hooks)

        def extra(k, carry):
            item(w0 + k, q, False)
            return carry
        lax.fori_loop(1, bn[b], extra, 0)

    assert nblk >= 2 and ring == 3
    for j in range(3):
        src_dma(j, j).start()
    start_weights(wexp[0], 0)
    for j in range(2):
        src_dma(j, j).wait()
        issue_gather(j)

    step(0)
    lax.fori_loop(1, nblk, lambda b, c: (step(b), c)[1], 0)

    last = nblk - 1
    dst_dma(last, last % ring).wait()
    issue_scatter(last % ring)
    src_dma(last + 3, last % ring).wait()
    wait_gather((last + 1) % ring)
    wait_gather((last + 2) % ring)
    for j in range(max(0, nblk - ring), nblk):
        wait_scatter(j % ring)


def _moe_experts(x1p, plan, w_gate, w_up, w_down, n_out_rows):
    src, dst = plan[-2:]
    ne, d, ff = w_gate.shape
    assert x1p.shape[1] * 2 == d and x1p.dtype == jnp.uint32
    anyspec = pl.BlockSpec(memory_space=pl.ANY)
    grid_spec = pltpu.PrefetchScalarGridSpec(
        num_scalar_prefetch=7,
        grid=(1,),
        in_specs=[anyspec] * 6,
        out_specs=anyspec,
        scratch_shapes=[
            pltpu.VMEM((MOE_RING, MOE_BLK, d // 2), jnp.uint32),
            pltpu.VMEM((MOE_RING, MOE_BLK, d // 2), jnp.uint32),
            pltpu.VMEM((MOE_BLK, d), BF16),
            pltpu.VMEM((2, d, ff), F32),
            pltpu.VMEM((2, d, ff), F32),
            pltpu.VMEM((2, ff, d), F32),
            pltpu.VMEM((d, ff), BF16),
            pltpu.VMEM((d, ff), BF16),
            pltpu.VMEM((ff, d), BF16),
            pltpu.SMEM((MOE_RING, MOE_BLK), jnp.int32),
            pltpu.SMEM((MOE_RING, MOE_BLK), jnp.int32),
            pltpu.SemaphoreType.DMA((MOE_RING,)),
            pltpu.SemaphoreType.DMA((MOE_RING,)),
            pltpu.SemaphoreType.DMA((2, MOE_RING)),
            pltpu.SemaphoreType.DMA((2, 3)),
        ],
    )
    return pl.pallas_call(
        _moe_kernel,
        grid_spec=grid_spec,
        out_shape=jax.ShapeDtypeStruct((n_out_rows, d // 2), jnp.uint32),
        compiler_params=_params(("arbitrary",)),
        name="moe_experts",
    )(*plan[:-2], src, dst, x1p, w_gate, w_up, w_down)


def _dispatch_plan(eidx, counts_f):
    t = eidx.shape[0]
    ne = counts_f.shape[-1]
    n = t * TOP_K
    blk = MOE_BLK
    assert n % blk == 0
    nblk = n // blk
    nwmax = nblk + ne - 1
    bits = max(n - 1, 1).bit_length()
    flat = jnp.arange(n, dtype=jnp.int32)
    skeys = jnp.sort(eidx.reshape(-1) * (1 << bits) + flat)
    order = skeys & ((1 << bits) - 1)
    tok = order // TOP_K
    src = tok.reshape(nblk, blk)
    dst = ((order % TOP_K) * t + tok).reshape(nblk, blk)

    counts = counts_f.reshape(ne).astype(jnp.int32)
    ends = jnp.cumsum(counts)
    starts = ends - counts
    fb = starts // blk
    nit = jnp.where(counts > 0, (ends + blk - 1) // blk - fb, 0)
    icum = jnp.cumsum(nit)
    nwork = icum[-1]
    eids = jnp.arange(ne, dtype=jnp.int32)
    nxt = jnp.flip(lax.cummin(jnp.flip(jnp.where(counts > 0, eids, ne)), axis=0))
    nxt_after = jnp.concatenate([nxt[1:], jnp.full((1,), ne, jnp.int32)])
    used_rank = jnp.cumsum((counts > 0).astype(jnp.int32)) - 1

    w = jnp.arange(nwmax, dtype=jnp.int32)
    onehot = (w[:, None] >= (icum - nit)[None, :]) & (w[:, None] < icum[None, :])
    pick = lambda v: jnp.sum(jnp.where(onehot, v[None, :], 0), axis=1)
    we = pick(eids)
    wb = pick(fb) + w - pick(icum - nit)
    lo = jnp.clip(pick(starts) - wb * blk, 0, blk)
    hi = jnp.clip(pick(ends) - wb * blk, 0, blk)
    wn = pick(nxt_after)
    wb = jnp.where(w < nwork, wb, nblk)
    prev_e = jnp.concatenate([jnp.full((1,), -1, jnp.int32), we[:-1]])
    flags = (jnp.where(we != prev_e, FLAG_NEW_EXPERT, 0)
             + jnp.where(wn < ne, FLAG_HAS_NEXT_EXPERT, 0)
             + (pick(used_rank) % 2) * FLAG_WEIGHT_SLOT)
    wn = jnp.minimum(wn, ne - 1)
    in_block = wb[None, :] == jnp.arange(nblk, dtype=jnp.int32)[:, None]
    bn = jnp.sum(in_block.astype(jnp.int32), axis=1)
    bw0 = jnp.cumsum(bn) - bn
    i32 = lambda v: v.astype(jnp.int32)
    return (i32(bw0), i32(bn), i32(we), i32(lo), i32(hi), i32(flags), i32(wn), src, dst)


def _final_kernel(ntp, *refs):
    x1_ref, gate_ref = refs[0], refs[1]
    y_refs = refs[2:2 + TOP_K]
    (wsg_hbm, wsu_hbm, wsd_hbm, g2_ref, b2_ref, outp_ref, outs_ref,
     wsg, wsu, wsd, stage_a, stage_b, sem) = refs[2 + TOP_K:]

    @pl.when(pl.program_id(0) == 0)
    def _():
        _load_bf16(wsg_hbm, wsg, stage_a, sem, stage_a.shape[0])
        _load_bf16(wsu_hbm, wsu, stage_a, sem, stage_a.shape[0])
        _load_bf16(wsd_hbm, wsd, stage_b, sem, stage_b.shape[0])

    x1 = x1_ref[...]
    xb = x1.astype(BF16)
    hact = _silu(_dot(xb, wsg[...])) * _dot(xb, wsu[...])
    f = _dot(hact.astype(BF16), wsd[...])
    gates = gate_ref[...]
    half_d = x1.shape[1] // 2
    routed_hi = jnp.zeros((x1.shape[0], half_d), F32)
    routed_lo = jnp.zeros((x1.shape[0], half_d), F32)
    for kk in range(TOP_K):
        hi, lo = _unpack_bf16_pair(y_refs[kk][...])
        gk = gates[:, kk:kk + 1]
        routed_hi = routed_hi + hi * gk
        routed_lo = routed_lo + lo * gk
    routed = jnp.concatenate([routed_hi, routed_lo], axis=1)
    out = _layernorm(DEEPNORM_ALPHA * x1 + (f + routed), g2_ref[...], b2_ref[...])

    @pl.when(pl.program_id(0) < ntp)
    def _():
        outp_ref[...] = out

    @pl.when(pl.program_id(0) >= ntp)
    def _():
        outs_ref[...] = out


def _final(x1, gates, ycomb, w_sg, w_su, w_sd, g2, b2, t_prompt):
    t = x1.shape[0]
    tm = ROW_TILE
    assert t % tm == 0 and t_prompt % tm == 0
    nt = t // tm
    ntp = t_prompt // tm
    ff = w_sg.shape[1]
    const = lambda shape: pl.BlockSpec(shape, lambda i: tuple(0 for _ in shape))
    yspec = lambda kk: pl.BlockSpec((tm, D_MODEL // 2), lambda i: (kk * nt + i, 0))
    anyspec = pl.BlockSpec(memory_space=pl.ANY)
    return pl.pallas_call(
        functools.partial(_final_kernel, ntp),
        grid=(nt,),
        in_specs=[pl.BlockSpec((tm, D_MODEL), lambda i: (i, 0)),
                  pl.BlockSpec((tm, LANES), lambda i: (i, 0))]
                 + [yspec(kk) for kk in range(TOP_K)]
                 + [anyspec, anyspec, anyspec, const((1, D_MODEL)), const((1, D_MODEL))],
        out_specs=[pl.BlockSpec((tm, D_MODEL), lambda i: (jnp.minimum(i, ntp - 1), 0)),
                   pl.BlockSpec((tm, D_MODEL), lambda i: (jnp.maximum(i - ntp, 0), 0))],
        out_shape=[jax.ShapeDtypeStruct((t_prompt, D_MODEL), F32),
                   jax.ShapeDtypeStruct((t - t_prompt, D_MODEL), F32)],
        scratch_shapes=[pltpu.VMEM((D_MODEL, ff), BF16),
                        pltpu.VMEM((D_MODEL, ff), BF16),
                        pltpu.VMEM((ff, D_MODEL), BF16),
                        pltpu.VMEM((512, ff), F32),
                        pltpu.VMEM((128, D_MODEL), F32),
                        pltpu.SemaphoreType.DMA(())],
        compiler_params=_params(("arbitrary",)),
        name="shared_combine_ln",
    )(x1, gates, *([ycomb] * TOP_K), w_sg, w_su, w_sd, g2, b2)


def kernel(x_prompt, x_sample, cache_k_win, cache_v_win, state_gla, w_in, attn_sinks, attn_norm_g, w_gla_a2,
           b_gla_a, gla_norm_g, w_out, ln1_g, ln1_b, w_router, router_bias, w_exp_gate, w_exp_up, w_exp_down,
           w_sh_gate, w_sh_up, w_sh_down, ln2_g, ln2_b):
    assert w_in.shape[0] == DEPTH == 1 and x_prompt.shape[0] == 1 and x_sample.shape[1] == 1
    t = x_prompt.shape[1]
    bsz = x_sample.shape[0]
    win = cache_k_win.shape[2]
    assert t % WINDOW == 0 and win == WINDOW

    wag = jnp.pad(w_in[0][:, MAIN_COLS:], ((0, 0), (0, LANES - GLA_RANK)))
    w_in0 = w_in[0].astype(BF16)
    wa2 = jnp.pad(w_gla_a2[0], ((0, LANES - GLA_RANK), (0, 0)))
    ba = b_gla_a[0].reshape(1, GLA_QK_COLS)
    sinks = jnp.pad(attn_sinks[0].reshape(1, N_HEADS), ((0, 0), (0, LANES - N_HEADS)))
    ag = attn_norm_g[0].reshape(1, Q_COLS)
    gn = gla_norm_g[0].reshape(1, GLA_WIDTH)

    xp = x_prompt[0]
    hp, lap = _inproj(xp, w_in0, wag, wa2, ba)
    cp, sap, sbp = _rope_tables(jnp.arange(t, dtype=jnp.int32))
    oa_p, nk_p, nv_p = _swa_prompt(hp, cp, sap, sbp, attn_sinks[0], ag)
    og_p, s_p = _gla_prompt(hp, lap, gn)

    xs = x_sample[:, 0]
    hs, las = _inproj(xs, w_in0, wag, wa2, ba)
    cs, sas, sbs = _rope_tables(PAST_LEN + jnp.arange(1, dtype=jnp.int32))
    lane_head = jnp.arange(Q_COLS, dtype=jnp.int32)[:, None] // HEAD_DIM
    emat = (lane_head == jnp.arange(LANES, dtype=jnp.int32)[None, :]).astype(BF16)
    hs3 = hs.reshape(bsz, 1, MAIN_COLS)
    oa_s, nk_s, nv_s = _swa_decode(hs3, cache_k_win[0].reshape(bsz, win, KV_COLS),
                                   cache_v_win[0].reshape(bsz, win, KV_COLS), cs, sas, sbs, sinks, ag,
                                   emat, emat.T)
    og_s, s_s = _gla_decode(hs3, las.reshape(bsz, 1, GLA_QK_COLS), state_gla[0], gn)

    t_all = t + bsz
    x1, x1p, eidx, gates, counts = _outproj_router(
        xp, xs, oa_p, oa_s.reshape(bsz, Q_COLS), og_p, og_s.reshape(bsz, GLA_WIDTH), w_out[0],
        ln1_g[0].reshape(1, D_MODEL), ln1_b[0].reshape(1, D_MODEL), w_router[0], router_bias[0].reshape(1, -1))
    plan = _dispatch_plan(eidx[:, :TOP_K], counts)
    ycomb = _moe_experts(x1p, plan, w_exp_gate[0], w_exp_up[0], w_exp_down[0], TOP_K * t_all)
    y_p, y_s = _final(x1, gates, ycomb, w_sh_gate[0], w_sh_up[0], w_sh_down[0],
                      ln2_g[0].reshape(1, D_MODEL), ln2_b[0].reshape(1, D_MODEL), t)

    y_prompt = y_p.reshape(1, t, D_MODEL)
    y_sample = y_s.reshape(bsz, 1, D_MODEL)
    new_k_prompt = nk_p.reshape(1, 1, win, N_KV_HEADS, HEAD_DIM)
    new_v_prompt = nv_p.reshape(1, 1, win, N_KV_HEADS, HEAD_DIM)
    new_s_prompt = s_p.reshape(1, 1, GLA_HEADS, GLA_DK, GLA_DV)
    new_k_sample = nk_s.reshape(1, bsz, win, N_KV_HEADS, HEAD_DIM)
    new_v_sample = nv_s.reshape(1, bsz, win, N_KV_HEADS, HEAD_DIM)
    new_s_sample = s_s.reshape(1, bsz, GLA_HEADS, GLA_DK, GLA_DV)
    return (y_prompt, y_sample, new_k_prompt, new_v_prompt, new_s_prompt,
            new_k_sample, new_v_sample, new_s_sample)
```

```python
import functools

import jax
import jax.numpy as jnp
from jax import lax
from jax.experimental import pallas as pl
from jax.experimental.pallas import tpu as pltpu

F32 = jnp.float32
BF16 = jnp.bfloat16

D_MODEL = 2048
PAST_LEN = 16384
HEAD_DIM = 64
N_HEADS = 16
N_KV_HEADS = 4
WINDOW = 128
ROPE_DIM = 16
ROPE_THETA = 500000.0
GLA_HEADS = 4
GLA_DV = 256
GLA_DK = 128
GLA_RANK = 16
GLA_TAU = 16.0
Q_COLS = N_HEADS * HEAD_DIM
KV_COLS = N_KV_HEADS * HEAD_DIM
GLA_QK_COLS = GLA_HEADS * GLA_DK
GLA_WIDTH = GLA_HEADS * GLA_DV
MAIN_COLS = Q_COLS + 2 * KV_COLS + 2 * GLA_QK_COLS + 2 * GLA_WIDTH
TOP_K = 8
N_GROUPS = 8
TOPK_GROUPS = 4
ROUTED_SCALE = 2.5
DEPTH = 1
DEEPNORM_ALPHA = (2 * DEPTH) ** 0.25
LN_EPS = 1e-5
RMS_EPS = 1e-6

LANES = 128
COL_BLOCK = 512
GLA_CHUNK = 64
GLA_SUB = 16
EXP_CLAMP = 80.0
ROW_TILE = 128
OUT_TILE = 256
MOE_BLK = 128
MOE_RING = 3
VMEM_LIMIT = 56 * 1024 * 1024


def _pick(n, cands):
    for c in cands:
        if n % c == 0:
            return c
    raise ValueError(f"no tile in {cands} divides {n}")


def _params(sem):
    return pltpu.CompilerParams(dimension_semantics=sem, vmem_limit_bytes=VMEM_LIMIT)


def _dot(a, b):
    return jnp.dot(a, b, preferred_element_type=F32)


def _dot_nt(a, b):
    return lax.dot_general(a, b, (((1,), (1,)), ((), ())), preferred_element_type=F32)


def _dot_tn(a, b):
    return lax.dot_general(a, b, (((0,), (0,)), ((), ())), preferred_element_type=F32)


def _split3(x):
    hi = x.astype(BF16)
    r1 = x - hi.astype(F32)
    mid = r1.astype(BF16)
    lo = (r1 - mid.astype(F32)).astype(BF16)
    return hi, mid, lo


def _pack_bf16_pair(a, b):
    hi = pltpu.bitcast(a.astype(BF16).astype(F32), jnp.uint32)
    lo = pltpu.bitcast(b.astype(BF16).astype(F32), jnp.uint32)
    return hi | lax.shift_right_logical(lo, jnp.uint32(16))


def _unpack_bf16_pair(w):
    hi = pltpu.bitcast(w & jnp.uint32(0xFFFF0000), F32)
    lo = pltpu.bitcast(lax.shift_left(w, jnp.uint32(16)), F32)
    return hi, lo


def _tile_lanes(t, width):
    reps = width // t.shape[-1]
    return t if reps == 1 else jnp.concatenate([t] * reps, axis=-1)


def _rope(x, c, sa, sb):
    w = x.shape[-1]
    return (x * _tile_lanes(c, w) + pltpu.roll(x, 8, 1) * _tile_lanes(sa, w)
            + pltpu.roll(x, w - 8, 1) * _tile_lanes(sb, w))


def _rope_tables(pos):
    half = ROPE_DIM // 2
    inv = 1.0 / (ROPE_THETA ** (jnp.arange(0, ROPE_DIM, 2, dtype=F32) / ROPE_DIM))
    ang = pos.astype(F32)[:, None] * inv[None, :]
    cos, sin = jnp.cos(ang), jnp.sin(ang)
    n = pos.shape[0]
    rest = HEAD_DIM - ROPE_DIM
    c = jnp.concatenate([cos, cos, jnp.ones((n, rest), F32)], axis=1)
    sa = jnp.concatenate([jnp.zeros((n, half), F32), sin, jnp.zeros((n, rest), F32)], axis=1)
    sb = jnp.concatenate([-sin, jnp.zeros((n, half + rest), F32)], axis=1)
    rep = LANES // HEAD_DIM
    return jnp.tile(c, (1, rep)), jnp.tile(sa, (1, rep)), jnp.tile(sb, (1, rep))


def _inproj_kernel(x_ref, w_ref, wag_ref, wa2_ref, ba_ref, h_ref, la_ref, xb_ref):
    @pl.when(pl.program_id(1) == 0)
    def _():
        xb = x_ref[...].astype(BF16)
        xb_ref[...] = xb
        ag = _dot(xb, wag_ref[...].astype(BF16))
        z = _dot(ag.astype(BF16), wa2_ref[...].astype(BF16)) + ba_ref[...]
        la_ref[...] = (jnp.minimum(z, 0.0) - jnp.log1p(jnp.exp(-jnp.abs(z)))) * (1.0 / GLA_TAU)

    h_ref[...] = _dot(xb_ref[...], w_ref[...])


def _inproj(x, w_in, wag, wa2, ba):
    t = x.shape[0]
    tm = _pick(t, (1024, 512, 256, 128))
    ncol = MAIN_COLS // COL_BLOCK
    return pl.pallas_call(
        _inproj_kernel,
        grid=(t // tm, ncol),
        in_specs=[
            pl.BlockSpec((tm, D_MODEL), lambda i, j: (i, 0)),
            pl.BlockSpec((D_MODEL, COL_BLOCK), lambda i, j: (0, j)),
            pl.BlockSpec((D_MODEL, LANES), lambda i, j: (0, 0)),
            pl.BlockSpec((LANES, GLA_QK_COLS), lambda i, j: (0, 0)),
            pl.BlockSpec((1, GLA_QK_COLS), lambda i, j: (0, 0)),
        ],
        out_specs=[
            pl.BlockSpec((tm, COL_BLOCK), lambda i, j: (i, j)),
            pl.BlockSpec((tm, GLA_QK_COLS), lambda i, j: (i, 0)),
        ],
        out_shape=[
            jax.ShapeDtypeStruct((t, MAIN_COLS), F32),
            jax.ShapeDtypeStruct((t, GLA_QK_COLS), F32),
        ],
        scratch_shapes=[pltpu.VMEM((tm, D_MODEL), BF16)],
        compiler_params=_params(("arbitrary", "arbitrary")),
        name="inproj",
    )(x, w_in, wag, wa2, ba)


def _half_variants(x):
    lane = lax.broadcasted_iota(jnp.int32, (x.shape[0], LANES), 1)
    lo = lane < HEAD_DIM
    out = []
    for g in range(N_KV_HEADS):
        chunk = x[:, (g // 2) * LANES:(g // 2 + 1) * LANES]
        swapped = pltpu.roll(chunk, HEAD_DIM, 1)
        u = g % 2
        for b in range(2):
            src = chunk if b == u else swapped
            keep = lo if b == 0 else jnp.logical_not(lo)
            out.append(jnp.where(keep, src, 0.0).astype(BF16))
    return out


def _swa_kernel(q_ref, kv_ref, c_ref, sa_ref, sb_ref, sink_ref, g_ref,
                oa_ref, nk_ref, nv_ref, kp_ref, vp_ref):
    i = pl.program_id(0)
    blk = q_ref.shape[0]
    c, sa, sb = c_ref[...], sa_ref[...], sb_ref[...]

    @pl.when(i == 0)
    def _():
        kp_ref[...] = jnp.zeros_like(kp_ref)
        vp_ref[...] = jnp.zeros_like(vp_ref)

    q = (_rope(q_ref[...], c, sa, sb) * (HEAD_DIM ** -0.5)).astype(BF16)
    kv = kv_ref[...]
    k = _rope(kv[:, :KV_COLS], c, sa, sb)
    v = kv[:, KV_COLS:]
    kc = _half_variants(k)
    vc = _half_variants(v)

    row = lax.broadcasted_iota(jnp.int32, (blk, 2 * blk), 0)
    col = lax.broadcasted_iota(jnp.int32, (blk, 2 * blk), 1)
    first_key = jnp.where(i > 0, 0, blk)
    mask = (col > row) & (col <= row + WINDOW) & (col >= first_key)

    pairs = []
    for p in range(N_HEADS // 2):
        qp = q[:, p * LANES:(p + 1) * LANES]
        acc = jnp.zeros((blk, LANES), F32)
        for b in range(2):
            h = 2 * p + b
            g = h // (N_HEADS // N_KV_HEADS)
            kcat = jnp.concatenate([kp_ref[2 * g + b], kc[2 * g + b]], axis=0)
            vcat = jnp.concatenate([vp_ref[2 * g + b], vc[2 * g + b]], axis=0)
            s = jnp.where(mask, _dot_nt(qp, kcat), -jnp.inf)
            sk = sink_ref[h]
            m = jnp.maximum(jnp.max(s, axis=1, keepdims=True), sk)
            e = jnp.exp(s - m)
            den = jnp.sum(e, axis=1, keepdims=True) + jnp.exp(sk - m)
            acc = acc + _dot((e / den).astype(BF16), vcat)
        pairs.append(acc)
    o = jnp.concatenate(pairs, axis=1)
    ms = jnp.mean(o * o, axis=1, keepdims=True)
    oa_ref[...] = (o * lax.rsqrt(ms + RMS_EPS) * g_ref[...]).astype(oa_ref.dtype)

    for n in range(2 * N_KV_HEADS):
        kp_ref[n] = kc[n]
        vp_ref[n] = vc[n]

    @pl.when(i == pl.num_programs(0) - 1)
    def _():
        nk_ref[...] = k
        nv_ref[...] = v


def _swa_prompt(h, c, sa, sb, sinks, g):
    t = h.shape[0]
    blk = WINDOW
    nb = t // blk
    return pl.pallas_call(
        _swa_kernel,
        grid=(nb,),
        in_specs=[
            pl.BlockSpec((blk, Q_COLS), lambda i: (i, 0)),
            pl.BlockSpec((blk, 2 * KV_COLS), lambda i: (i, Q_COLS // (2 * KV_COLS))),
            pl.BlockSpec((blk, LANES), lambda i: (i, 0)),
            pl.BlockSpec((blk, LANES), lambda i: (i, 0)),
            pl.BlockSpec((blk, LANES), lambda i: (i, 0)),
            pl.BlockSpec(memory_space=pltpu.SMEM),
            pl.BlockSpec((1, Q_COLS), lambda i: (0, 0)),
        ],
        out_specs=[
            pl.BlockSpec((blk, Q_COLS), lambda i: (i, 0)),
            pl.BlockSpec((blk, KV_COLS), lambda i: (0, 0)),
            pl.BlockSpec((blk, KV_COLS), lambda i: (0, 0)),
        ],
        out_shape=[
            jax.ShapeDtypeStruct((t, Q_COLS), BF16),
            jax.ShapeDtypeStruct((blk, KV_COLS), F32),
            jax.ShapeDtypeStruct((blk, KV_COLS), F32),
        ],
        scratch_shapes=[pltpu.VMEM((2 * N_KV_HEADS, blk, LANES), BF16),
                        pltpu.VMEM((2 * N_KV_HEADS, blk, LANES), BF16)],
        compiler_params=_params(("arbitrary",)),
        name="swa_prompt",
    )(h, h, c, sa, sb, sinks, g)


def _silu(x):
    return x * jax.nn.sigmoid(x)


def _gla_kernel(qg_ref, kg_ref, v1_ref, v2_ref, r1_ref, r2_ref, la_ref, gn_ref,
                og_ref, sfin_ref, s_ref):
    i = pl.program_id(0)
    rows = qg_ref.shape[0]
    ch = GLA_CHUNK
    nsub = ch // GLA_SUB

    @pl.when(i == 0)
    def _():
        s_ref[...] = jnp.zeros_like(s_ref)

    rr = lax.broadcasted_iota(jnp.int32, (ch, ch), 0)
    cc = lax.broadcasted_iota(jnp.int32, (ch, ch), 1)
    tri = jnp.where(rr >= cc, 1.0, 0.0).astype(BF16)
    pre = jnp.where(cc < (rr // GLA_SUB) * GLA_SUB, 1.0, 0.0).astype(BF16)
    mr = lax.broadcasted_iota(jnp.int32, (ch, nsub * ch), 0)
    mc = lax.broadcasted_iota(jnp.int32, (ch, nsub * ch), 1)
    amask = ((mc // ch) == (mr // GLA_SUB)) & ((mc % ch) <= mr)
    ones_t = jnp.ones((ch, LANES), BF16)

    def chunk(ci, carry):
        r0 = pl.multiple_of(ci * ch, ch)
        hi, mid, lo = _split3(la_ref[pl.ds(r0, ch), :])
        b_all = _dot(tri, hi) + _dot(tri, mid) + _dot(tri, lo)
        bs_all = _dot(pre, hi) + _dot(pre, mid) + _dot(pre, lo)
        bl_col_all = _dot_tn(hi, ones_t) + _dot_tn(mid, ones_t) + _dot_tn(lo, ones_t)
        for hh in range(GLA_HEADS):
            dk = slice(hh * GLA_DK, (hh + 1) * GLA_DK)
            vref, rref = (v1_ref, r1_ref) if hh < 2 else (v2_ref, r2_ref)
            dv = slice((hh % 2) * GLA_DV, (hh % 2 + 1) * GLA_DV)
            q = qg_ref[pl.ds(r0, ch), dk] * (GLA_DK ** -0.5)
            k = kg_ref[pl.ds(r0, ch), dk]
            v = vref[pl.ds(r0, ch), dv]
            r = rref[pl.ds(r0, ch), dv]
            b = b_all[:, dk]
            bs = bs_all[:, dk]
            blast = b[ch - 1:ch, :]
            bl_col = bl_col_all[hh * GLA_DK:(hh + 1) * GLA_DK, :]
            dec_col = jnp.exp(jnp.concatenate([bl_col, bl_col], axis=1))

            qt = (q * jnp.exp(b - bs)).astype(BF16)
            kparts = []
            for si in range(nsub):
                ref_row = bs[si * GLA_SUB:si * GLA_SUB + 1, :]
                kparts.append((k * jnp.exp(jnp.minimum(ref_row - b, EXP_CLAMP))).astype(BF16))
            kcat = jnp.concatenate(kparts, axis=0)
            a = jnp.where(amask, _dot_nt(qt, kcat), 0.0).astype(BF16)
            vb = v.astype(BF16)
            vrep = jnp.concatenate([vb] * nsub, axis=0)
            s_old = s_ref[hh]
            o = _dot(a, vrep) + _dot((q * jnp.exp(b)).astype(BF16), s_old.astype(BF16))
            kd = (k * jnp.exp(blast - b)).astype(BF16)
            s_ref[hh] = dec_col * s_old + _dot_tn(kd, vb)

            ms = jnp.mean(o * o, axis=1, keepdims=True)
            gn = gn_ref[:, hh * GLA_DV:(hh + 1) * GLA_DV]
            y = o * lax.rsqrt(ms + RMS_EPS) * gn
            og_ref[pl.ds(r0, ch), hh * GLA_DV:(hh + 1) * GLA_DV] = (y * _silu(r)).astype(og_ref.dtype)
        return carry

    lax.fori_loop(0, rows // ch, chunk, 0)

    @pl.when(i == pl.num_programs(0) - 1)
    def _():
        sfin_ref[...] = s_ref[...]


def _gla_prompt(h, la, gn):
    t = h.shape[0]
    rows = _pick(t, (512, 256, 128, 64))
    cb = COL_BLOCK
    base = (Q_COLS + 2 * KV_COLS) // cb
    spec = lambda j: pl.BlockSpec((rows, cb), lambda i: (i, j))
    return pl.pallas_call(
        _gla_kernel,
        grid=(t // rows,),
        in_specs=[spec(base), spec(base + 1), spec(base + 2), spec(base + 3), spec(base + 4), spec(base + 5),
                  pl.BlockSpec((rows, GLA_QK_COLS), lambda i: (i, 0)),
                  pl.BlockSpec((1, GLA_WIDTH), lambda i: (0, 0))],
        out_specs=[
            pl.BlockSpec((rows, GLA_WIDTH), lambda i: (i, 0)),
            pl.BlockSpec((GLA_HEADS, GLA_DK, GLA_DV), lambda i: (0, 0, 0)),
        ],
        out_shape=[
            jax.ShapeDtypeStruct((t, GLA_WIDTH), BF16),
            jax.ShapeDtypeStruct((GLA_HEADS, GLA_DK, GLA_DV), F32),
        ],
        scratch_shapes=[pltpu.VMEM((GLA_HEADS, GLA_DK, GLA_DV), F32)],
        compiler_params=_params(("arbitrary",)),
        name="gla_prompt",
    )(h, h, h, h, h, h, la, gn)


def _dup_heads(x):
    lane = lax.broadcasted_iota(jnp.int32, (x.shape[0], LANES), 1)
    lo = lane < HEAD_DIM
    out = []
    for g in range(N_KV_HEADS):
        chunk = x[:, (g // 2) * LANES:(g // 2 + 1) * LANES]
        swapped = pltpu.roll(chunk, HEAD_DIM, 1)
        out.append(jnp.where(lo, chunk, swapped) if g % 2 == 0 else jnp.where(lo, swapped, chunk))
    return out


def _swa_dec_kernel(q_ref, kv_ref, ck_ref, cv_ref, c_ref, sa_ref, sb_ref, sink_ref, g_ref, e_ref, et_ref,
                    oa_ref, nk_ref, nv_ref):
    win = ck_ref.shape[1]
    c, sa, sb = c_ref[...], sa_ref[...], sb_ref[...]
    q8 = jnp.broadcast_to(q_ref[0], (8, Q_COLS))
    kv8 = jnp.broadcast_to(kv_ref[0], (8, 2 * KV_COLS))
    q = _rope(q8, c, sa, sb) * (HEAD_DIM ** -0.5)
    knew = _rope(kv8[:, :KV_COLS], c, sa, sb)
    vnew = kv8[:, KV_COLS:]

    row = lax.broadcasted_iota(jnp.int32, (win, KV_COLS), 0)
    last = row == win - 1
    keys = jnp.where(last, jnp.broadcast_to(knew[0:1], (win, KV_COLS)), pltpu.roll(ck_ref[0], win - 1, 0))
    vals = jnp.where(last, jnp.broadcast_to(vnew[0:1], (win, KV_COLS)), pltpu.roll(cv_ref[0], win - 1, 0))
    nk_ref[0] = keys
    nv_ref[0] = vals

    qb = q.astype(BF16).astype(F32)
    kd = _dup_heads(keys.astype(BF16).astype(F32))
    vd = _dup_heads(vals.astype(BF16).astype(F32))
    npair = N_HEADS // 2
    prod = jnp.concatenate(
        [kd[p // 2] * jnp.broadcast_to(qb[0:1, p * LANES:(p + 1) * LANES], (win, LANES)) for p in range(npair)],
        axis=1)
    st = _dot(prod.astype(BF16), e_ref[...])
    sk = sink_ref[...]
    m = jnp.maximum(jnp.max(st, axis=0, keepdims=True), sk)
    e = jnp.exp(st - m)
    den = jnp.sum(e, axis=0, keepdims=True) + jnp.exp(sk - m)
    pe = _dot((e / den).astype(BF16), et_ref[...])
    vcat = jnp.concatenate([vd[p // 2] for p in range(npair)], axis=1)
    o = jnp.sum(pe * vcat, axis=0, keepdims=True)
    ms = jnp.mean(o * o, axis=1, keepdims=True)
    oa_ref[0] = (o * lax.rsqrt(ms + RMS_EPS) * g_ref[...]).astype(oa_ref.dtype)


def _swa_decode(h3, ck, cv, c, sa, sb, sinks, g, e, et):
    bsz = h3.shape[0]
    win = ck.shape[1]
    const = lambda shape: pl.BlockSpec(shape, lambda b: tuple(0 for _ in shape))
    return pl.pallas_call(
        _swa_dec_kernel,
        grid=(bsz,),
        in_specs=[
            pl.BlockSpec((1, 1, Q_COLS), lambda b: (b, 0, 0)),
            pl.BlockSpec((1, 1, 2 * KV_COLS), lambda b: (b, 0, Q_COLS // (2 * KV_COLS))),
            pl.BlockSpec((1, win, KV_COLS), lambda b: (b, 0, 0)),
            pl.BlockSpec((1, win, KV_COLS), lambda b: (b, 0, 0)),
            const((1, LANES)), const((1, LANES)), const((1, LANES)), const((1, LANES)),
            const((1, Q_COLS)), const((Q_COLS, LANES)), const((LANES, Q_COLS)),
        ],
        out_specs=[
            pl.BlockSpec((1, 1, Q_COLS), lambda b: (b, 0, 0)),
            pl.BlockSpec((1, win, KV_COLS), lambda b: (b, 0, 0)),
            pl.BlockSpec((1, win, KV_COLS), lambda b: (b, 0, 0)),
        ],
        out_shape=[
            jax.ShapeDtypeStruct((bsz, 1, Q_COLS), BF16),
            jax.ShapeDtypeStruct((bsz, win, KV_COLS), F32),
            jax.ShapeDtypeStruct((bsz, win, KV_COLS), F32),
        ],
        compiler_params=_params(("arbitrary",)),
        name="swa_decode",
    )(h3, h3, ck, cv, c, sa, sb, sinks, g, e, et)


def _rows8(rows):
    w = rows[0].shape[1]
    ridx = lax.broadcasted_iota(jnp.int32, (8, w), 0)
    out = jnp.zeros((8, w), F32)
    for n, r in enumerate(rows):
        out = jnp.where(ridx == n, jnp.broadcast_to(r, (8, w)), out)
    return out


def _gla_dec_kernel(qg_ref, kg_ref, v1_ref, v2_ref, r1_ref, r2_ref, la_ref, s_ref, gn_ref, og_ref, sn_ref):
    ones8 = jnp.where(lax.broadcasted_iota(jnp.int32, (8, GLA_DV), 0) < 3, 1.0, 0.0).astype(BF16)
    outs = []
    for hh in range(GLA_HEADS):
        dk = slice(hh * GLA_DK, (hh + 1) * GLA_DK)
        vref, rref = (v1_ref, r1_ref) if hh < 2 else (v2_ref, r2_ref)
        dv = slice((hh % 2) * GLA_DV, (hh % 2 + 1) * GLA_DV)
        q = qg_ref[0][:, dk] * (GLA_DK ** -0.5)
        k = kg_ref[0][:, dk]
        la = la_ref[0][:, dk]
        v = vref[0][:, dv]
        r = rref[0][:, dv]
        s_old = s_ref[0, hh]
        dec = jnp.exp(la)
        qb = q.astype(BF16).astype(F32)
        kb = k.astype(BF16).astype(F32)
        score = jnp.sum(qb * kb, axis=1, keepdims=True)
        q8 = _rows8([q * dec]).astype(BF16)
        o = score * v + _dot(q8, s_old.astype(BF16))[0:1, :]
        dhi, dmid, dlo = _split3(dec)
        d8 = _rows8([dhi.astype(F32), dmid.astype(F32), dlo.astype(F32)]).astype(BF16)
        dec_col = _dot_tn(d8, ones8)
        k8 = _rows8([k]).astype(BF16)
        v8 = _rows8([v]).astype(BF16)
        sn_ref[0, hh] = dec_col * s_old + _dot_tn(k8, v8)
        ms = jnp.mean(o * o, axis=1, keepdims=True)
        gn = gn_ref[:, hh * GLA_DV:(hh + 1) * GLA_DV]
        outs.append(o * lax.rsqrt(ms + RMS_EPS) * gn * _silu(r))
    og_ref[0] = jnp.concatenate(outs, axis=1).astype(og_ref.dtype)


def _gla_decode(h3, la3, s0, gn):
    bsz = h3.shape[0]
    cb = COL_BLOCK
    base = (Q_COLS + 2 * KV_COLS) // cb
    spec = lambda j: pl.BlockSpec((1, 1, cb), lambda b: (b, 0, j))
    return pl.pallas_call(
        _gla_dec_kernel,
        grid=(bsz,),
        in_specs=[spec(base), spec(base + 1), spec(base + 2), spec(base + 3), spec(base + 4), spec(base + 5),
                  pl.BlockSpec((1, 1, GLA_QK_COLS), lambda b: (b, 0, 0)),
                  pl.BlockSpec((1, GLA_HEADS, GLA_DK, GLA_DV), lambda b: (b, 0, 0, 0)),
                  pl.BlockSpec((1, GLA_WIDTH), lambda b: (0, 0))],
        out_specs=[
            pl.BlockSpec((1, 1, GLA_WIDTH), lambda b: (b, 0, 0)),
            pl.BlockSpec((1, GLA_HEADS, GLA_DK, GLA_DV), lambda b: (b, 0, 0, 0)),
        ],
        out_shape=[
            jax.ShapeDtypeStruct((bsz, 1, GLA_WIDTH), BF16),
            jax.ShapeDtypeStruct(s0.shape, s0.dtype),
        ],
        compiler_params=_params(("arbitrary",)),
        name="gla_decode",
    )(h3, h3, h3, h3, h3, h3, la3, s0, gn)


def _layernorm(v, g, b):
    mu = jnp.mean(v, axis=1, keepdims=True)
    d = v - mu
    var = jnp.mean(d * d, axis=1, keepdims=True)
    return d * lax.rsqrt(var + LN_EPS) * g + b


def _load_bf16(w_hbm, dst_ref, stage_ref, sem, chunk):
    for s in range(w_hbm.shape[0] // chunk):
        cp = pltpu.make_async_copy(w_hbm.at[pl.ds(s * chunk, chunk), :], stage_ref, sem)
        cp.start()
        cp.wait()
        dst_ref[pl.ds(s * chunk, chunk), :] = stage_ref[...].astype(BF16)


def _route(logits, bias, valid):
    rows, ne = logits.shape
    per = ne // N_GROUPS
    ninf = -jnp.inf
    lane = lax.broadcasted_iota(jnp.int32, (rows, ne), 1)
    gid = lane // per
    scores = jax.nn.sigmoid(logits)
    biased = scores + bias
    gfull = jnp.zeros((rows, ne), F32)
    gcols = []
    for g in range(N_GROUPS):
        ing = gid == g
        xg = jnp.where(ing, biased, ninf)
        m1 = jnp.max(xg, axis=1, keepdims=True)
        cnt = jnp.sum(jnp.where(xg == m1, 1.0, 0.0), axis=1, keepdims=True)
        m2 = jnp.max(jnp.where(xg < m1, xg, ninf), axis=1, keepdims=True)
        gs = m1 + jnp.where(cnt >= 2.0, m1, m2)
        gcols.append(gs)
        gfull = jnp.where(ing, gs, gfull)
    rank = jnp.zeros((rows, ne), jnp.int32)
    for g in range(N_GROUPS):
        beats = (gcols[g] > gfull) | ((gcols[g] == gfull) & (gid > g))
        rank = rank + jnp.where(beats, 1, 0)
    masked = jnp.where(rank < TOPK_GROUPS, biased, ninf)

    lane_f = lane.astype(F32)
    out_lane = lax.broadcasted_iota(jnp.int32, (rows, LANES), 1)
    eout = jnp.zeros((rows, LANES), F32)
    gout = jnp.zeros((rows, LANES), F32)
    total = jnp.zeros((rows, 1), F32)
    chosen = jnp.zeros((rows, ne), F32)
    for kk in range(TOP_K):
        m = jnp.max(masked, axis=1, keepdims=True)
        idx = jnp.min(jnp.where(masked == m, lane_f, float(ne)), axis=1, keepdims=True)
        hit = lane_f == idx
        gk = jnp.sum(jnp.where(hit, scores, 0.0), axis=1, keepdims=True)
        masked = jnp.where(hit, ninf, masked)
        chosen = jnp.where(hit, 1.0, chosen)
        total = total + gk
        eout = jnp.where(out_lane == kk, idx, eout)
        gout = jnp.where(out_lane == kk, gk, gout)
    counts = jnp.sum(jnp.where(valid, chosen, 0.0), axis=0, keepdims=True)
    return eout.astype(jnp.int32), gout / total * ROUTED_SCALE, counts


def _outproj_kernel(ntp, t_valid, xp_ref, xs_ref, oap_ref, oas_ref, ogp_ref, ogs_ref, wout_hbm, g1_ref, b1_ref,
                    wr_ref, rb_ref, x1_ref, x1p_ref, eidx_ref, gate_ref, cnt_ref, wo_ref, stage_ref, sem):
    @pl.when(pl.program_id(0) == 0)
    def _():
        _load_bf16(wout_hbm, wo_ref, stage_ref, sem, stage_ref.shape[0])
        cnt_ref[...] = jnp.zeros_like(cnt_ref)

    is_prompt = pl.program_id(0) < ntp
    x = jnp.where(is_prompt, xp_ref[...], xs_ref[...])
    oa = jnp.where(is_prompt, oap_ref[...], oas_ref[...])
    og = jnp.where(is_prompt, ogp_ref[...], ogs_ref[...])
    half = oa.shape[1]
    mix = _dot(oa, wo_ref[pl.ds(0, half), :]) + _dot(og, wo_ref[pl.ds(half, half), :])
    x1 = _layernorm(DEEPNORM_ALPHA * x + mix, g1_ref[...], b1_ref[...])
    x1_ref[...] = x1
    x1p_ref[...] = _pack_bf16_pair(x1[:, :D_MODEL // 2], x1[:, D_MODEL // 2:])
    logits = _dot(x1.astype(BF16), wr_ref[...].astype(BF16))
    tm = x1.shape[0]
    row = pl.program_id(0) * tm + lax.broadcasted_iota(jnp.int32, (tm, 1), 0)
    eidx, gates, counts = _route(logits, rb_ref[...], row < t_valid)
    eidx_ref[...] = eidx
    gate_ref[...] = gates
    cnt_ref[...] += counts


def _outproj_router(xp, xs, oap, oas, ogp, ogs, w_out, g1, b1, w_router, rbias):
    tm = OUT_TILE
    tp, ts = xp.shape[0], xs.shape[0]
    assert tp % tm == 0
    ntp = tp // tm
    pad = -ts % tm
    xs, oas, ogs = (jnp.pad(a, ((0, pad), (0, 0))) for a in (xs, oas, ogs))
    t = tp + ts + pad
    ne = w_router.shape[1]
    const = lambda shape: pl.BlockSpec(shape, lambda i: tuple(0 for _ in shape))
    prompt = lambda width: pl.BlockSpec((tm, width), lambda i: (jnp.minimum(i, ntp - 1), 0))
    sample = lambda width: pl.BlockSpec((tm, width), lambda i: (jnp.maximum(i - ntp, 0), 0))
    return pl.pallas_call(
        functools.partial(_outproj_kernel, ntp, tp + ts),
        grid=(t // tm,),
        in_specs=[
            prompt(D_MODEL), sample(D_MODEL), prompt(Q_COLS), sample(Q_COLS), prompt(GLA_WIDTH), sample(GLA_WIDTH),
            pl.BlockSpec(memory_space=pl.ANY),
            const((1, D_MODEL)), const((1, D_MODEL)), const((D_MODEL, ne)), const((1, ne)),
        ],
        out_specs=[
            pl.BlockSpec((tm, D_MODEL), lambda i: (i, 0)),
            pl.BlockSpec((tm, D_MODEL // 2), lambda i: (i, 0)),
            pl.BlockSpec((tm, LANES), lambda i: (i, 0)),
            pl.BlockSpec((tm, LANES), lambda i: (i, 0)),
            const((1, ne)),
        ],
        out_shape=[
            jax.ShapeDtypeStruct((t, D_MODEL), F32),
            jax.ShapeDtypeStruct((t, D_MODEL // 2), jnp.uint32),
            jax.ShapeDtypeStruct((t, LANES), jnp.int32),
            jax.ShapeDtypeStruct((t, LANES), F32),
            jax.ShapeDtypeStruct((1, ne), F32),
        ],
        scratch_shapes=[pltpu.VMEM((D_MODEL, D_MODEL), BF16),
                        pltpu.VMEM((512, D_MODEL), F32),
                        pltpu.SemaphoreType.DMA(())],
        compiler_params=_params(("arbitrary",)),
        name="outproj_router",
    )(xp, xs, oap, oas, ogp, ogs, w_out, g1, b1, w_router, rbias)


FLAG_NEW_EXPERT = 1
FLAG_HAS_NEXT_EXPERT = 2
FLAG_WEIGHT_SLOT = 4


def _moe_kernel(bw0, bn, wexp, wlo, whi, wflag, wnext,
                src_hbm, dst_hbm, x_hbm, wg_hbm, wu_hbm, wd_hbm, y_hbm,
                xbuf, ybuf, xbb, wgf, wuf, wdf, wgb, wub, wdb, srcs, dsts, gsem, ssem, isem, wsem):
    nblk = src_hbm.shape[0]
    ring, blk = xbuf.shape[:2]
    ff = wgb.shape[1]

    def slot_of(j):
        return j % ring if isinstance(j, int) else lax.rem(j, ring)

    def src_dma(j, q):
        j = min(j, nblk - 1) if isinstance(j, int) else jnp.minimum(j, nblk - 1)
        return pltpu.make_async_copy(src_hbm.at[j], srcs.at[q], isem.at[0, q])

    def dst_dma(j, q):
        return pltpu.make_async_copy(dst_hbm.at[j], dsts.at[q], isem.at[1, q])

    def gather_copy(tok, r, q):
        return pltpu.make_async_copy(x_hbm.at[pl.ds(tok, 1), :], xbuf.at[q, pl.ds(r, 1), :], gsem.at[q])

    def scatter_copy(dst, r, q):
        return pltpu.make_async_copy(ybuf.at[q, pl.ds(r, 1), :], y_hbm.at[pl.ds(dst, 1), :], ssem.at[q])

    def issue_gather(q, rows=None):
        for r in (range(blk) if rows is None else rows):
            gather_copy(srcs[q, r], r, q).start()

    def wait_gather(q):
        for r in range(blk):
            gather_copy(0, r, q).wait()

    def issue_scatter(q, rows=None):
        for r in (range(blk) if rows is None else rows):
            scatter_copy(dsts[q, r], r, q).start()

    def wait_scatter(q):
        for r in range(blk):
            scatter_copy(0, r, q).wait()

    def weight_copies(e, ws):
        return (pltpu.make_async_copy(wg_hbm.at[e], wgf.at[ws], wsem.at[ws, 0]),
                pltpu.make_async_copy(wu_hbm.at[e], wuf.at[ws], wsem.at[ws, 1]),
                pltpu.make_async_copy(wd_hbm.at[e], wdf.at[ws], wsem.at[ws, 2]))

    def start_weights(e, ws):
        for cp in weight_copies(e, ws):
            cp.start(priority=1)

    def cast_weights(ws):
        step = 64
        for c in range(0, wgb.shape[0], step):
            wgb[pl.ds(c, step), :] = wgf[ws, pl.ds(c, step), :].astype(BF16)
            wub[pl.ds(c, step), :] = wuf[ws, pl.ds(c, step), :].astype(BF16)
        step = 16
        for c in range(0, wdb.shape[0], step):
            wdb[pl.ds(c, step), :] = wdf[ws, pl.ds(c, step), :].astype(BF16)

    def item(w, q, first, hooks=None):
        fl = wflag[w]
        anchor = hooks is not None
        hooks = hooks or (lambda: None,) * 4

        @pl.when((fl & FLAG_NEW_EXPERT) != 0)
        def _():
            ws = jnp.where((fl & FLAG_WEIGHT_SLOT) != 0, 1, 0)
            for cp in weight_copies(0, ws):
                cp.wait()
            cast_weights(ws)

            @pl.when((fl & FLAG_HAS_NEXT_EXPERT) != 0)
            def _():
                start_weights(wnext[w], 1 - ws)

        def after_ring_dmas():
            probe = xbuf[q, pl.ds(0, 8), pl.ds(0, LANES)]
            zero = lax.shift_right_logical(lax.shift_right_logical(probe, jnp.uint32(16)), jnp.uint32(16))
            return pltpu.bitcast(zero, F32)[0:1, 0:1]

        hooks[0]()
        gate = _dot(xbb[...], wgb[...]) + (after_ring_dmas() if anchor else 0.0)
        hooks[1]()
        up = _dot(xbb[...], wub[...]) + (after_ring_dmas() if anchor else 0.0)
        hooks[2]()
        hact = _silu(gate) * up
        rows = lax.broadcasted_iota(jnp.int32, (blk, ff), 0)
        hact = jnp.where((rows >= wlo[w]) & (rows < whi[w]), hact, 0.0)
        y = _dot(hact.astype(BF16), wdb[...])
        hooks[3]()
        half_d = y.shape[1] // 2
        if first:
            ybuf[q] = _pack_bf16_pair(y[:, :half_d], y[:, half_d:])
        else:
            hi, lo = _unpack_bf16_pair(ybuf[q])
            ybuf[q] = _pack_bf16_pair(hi + y[:, :half_d], lo + y[:, half_d:])

    def step(b):
        static = isinstance(b, int)
        q = slot_of(b)
        wait_gather(q)
        if static:
            if b >= ring:
                wait_scatter(q)
        else:
            @pl.when(b >= ring)
            def _():
                wait_scatter(q)
        if not (static and b == 0):
            dst_dma(b - 1, slot_of(b + 2)).wait()
        src_dma(b + 2, slot_of(b + 2)).wait()
        src_dma(b + 3, q).start()
        dst_dma(b, q).start()
        hi, lo = _unpack_bf16_pair(xbuf[q])
        half_d = xbb.shape[1] // 2
        xbb[:, pl.ds(0, half_d)] = hi.astype(BF16)
        xbb[:, pl.ds(half_d, half_d)] = lo.astype(BF16)
        w0 = bw0[b]

        qn = slot_of(b + 2)
        half = blk // 2
        hooks = [lambda: issue_gather(qn, range(0, half)), lambda: issue_gather(qn, range(half, blk))]
        if static and b == 0:
            hooks += [lambda: None, lambda: None]
        else:
            hooks += [lambda: issue_scatter(qn, range(0, half)), lambda: issue_scatter(qn, range(half, blk))]
        item(w0, q, True, hooks)

        def extra(k, carry):
            item(w0 + k, q, False)
            return carry
        lax.fori_loop(1, bn[b], extra, 0)

    assert nblk >= 2 and ring == 3
    for j in range(3):
        src_dma(j, j).start()
    start_weights(wexp[0], 0)
    for j in range(2):
        src_dma(j, j).wait()
        issue_gather(j)

    step(0)
    lax.fori_loop(1, nblk, lambda b, c: (step(b), c)[1], 0)

    last = nblk - 1
    dst_dma(last, last % ring).wait()
    issue_scatter(last % ring)
    src_dma(last + 3, last % ring).wait()
    wait_gather((last + 1) % ring)
    wait_gather((last + 2) % ring)
    for j in range(max(0, nblk - ring), nblk):
        wait_scatter(j % ring)


def _moe_experts(x1p, plan, w_gate, w_up, w_down, n_out_rows):
    src, dst = plan[-2:]
    ne, d, ff = w_gate.shape
    assert x1p.shape[1] * 2 == d and x1p.dtype == jnp.uint32
    anyspec = pl.BlockSpec(memory_space=pl.ANY)
    grid_spec = pltpu.PrefetchScalarGridSpec(
        num_scalar_prefetch=7,
        grid=(1,),
        in_specs=[anyspec] * 6,
        out_specs=anyspec,
        scratch_shapes=[
            pltpu.VMEM((MOE_RING, MOE_BLK, d // 2), jnp.uint32),
            pltpu.VMEM((MOE_RING, MOE_BLK, d // 2), jnp.uint32),
            pltpu.VMEM((MOE_BLK, d), BF16),
            pltpu.VMEM((2, d, ff), F32),
            pltpu.VMEM((2, d, ff), F32),
            pltpu.VMEM((2, ff, d), F32),
            pltpu.VMEM((d, ff), BF16),
            pltpu.VMEM((d, ff), BF16),
            pltpu.VMEM((ff, d), BF16),
            pltpu.SMEM((MOE_RING, MOE_BLK), jnp.int32),
            pltpu.SMEM((MOE_RING, MOE_BLK), jnp.int32),
            pltpu.SemaphoreType.DMA((MOE_RING,)),
            pltpu.SemaphoreType.DMA((MOE_RING,)),
            pltpu.SemaphoreType.DMA((2, MOE_RING)),
            pltpu.SemaphoreType.DMA((2, 3)),
        ],
    )
    return pl.pallas_call(
        _moe_kernel,
        grid_spec=grid_spec,
        out_shape=jax.ShapeDtypeStruct((n_out_rows, d // 2), jnp.uint32),
        compiler_params=_params(("arbitrary",)),
        name="moe_experts",
    )(*plan[:-2], src, dst, x1p, w_gate, w_up, w_down)


def _dispatch_plan(eidx, counts_f):
    t = eidx.shape[0]
    ne = counts_f.shape[-1]
    n = t * TOP_K
    blk = MOE_BLK
    assert n % blk == 0
    nblk = n // blk
    nwmax = nblk + ne - 1
    bits = max(n - 1, 1).bit_length()
    flat = jnp.arange(n, dtype=jnp.int32)
    skeys = jnp.sort(eidx.reshape(-1) * (1 << bits) + flat)
    order = skeys & ((1 << bits) - 1)
    tok = order // TOP_K
    src = tok.reshape(nblk, blk)
    dst = ((order % TOP_K) * t + tok).reshape(nblk, blk)

    counts = counts_f.reshape(ne).astype(jnp.int32)
    ends = jnp.cumsum(counts)
    starts = ends - counts
    fb = starts // blk
    nit = jnp.where(counts > 0, (ends + blk - 1) // blk - fb, 0)
    icum = jnp.cumsum(nit)
    nwork = icum[-1]
    eids = jnp.arange(ne, dtype=jnp.int32)
    nxt = jnp.flip(lax.cummin(jnp.flip(jnp.where(counts > 0, eids, ne)), axis=0))
    nxt_after = jnp.concatenate([nxt[1:], jnp.full((1,), ne, jnp.int32)])
    used_rank = jnp.cumsum((counts > 0).astype(jnp.int32)) - 1

    w = jnp.arange(nwmax, dtype=jnp.int32)
    onehot = (w[:, None] >= (icum - nit)[None, :]) & (w[:, None] < icum[None, :])
    pick = lambda v: jnp.sum(jnp.where(onehot, v[None, :], 0), axis=1)
    we = pick(eids)
    wb = pick(fb) + w - pick(icum - nit)
    lo = jnp.clip(pick(starts) - wb * blk, 0, blk)
    hi = jnp.clip(pick(ends) - wb * blk, 0, blk)
    wn = pick(nxt_after)
    wb = jnp.where(w < nwork, wb, nblk)
    prev_e = jnp.concatenate([jnp.full((1,), -1, jnp.int32), we[:-1]])
    flags = (jnp.where(we != prev_e, FLAG_NEW_EXPERT, 0)
             + jnp.where(wn < ne, FLAG_HAS_NEXT_EXPERT, 0)
             + (pick(used_rank) % 2) * FLAG_WEIGHT_SLOT)
    wn = jnp.minimum(wn, ne - 1)
    in_block = wb[None, :] == jnp.arange(nblk, dtype=jnp.int32)[:, None]
    bn = jnp.sum(in_block.astype(jnp.int32), axis=1)
    bw0 = jnp.cumsum(bn) - bn
    i32 = lambda v: v.astype(jnp.int32)
    return (i32(bw0), i32(bn), i32(we), i32(lo), i32(hi), i32(flags), i32(wn), src, dst)


def _final_kernel(ntp, *refs):
    x1_ref, gate_ref = refs[0], refs[1]
    y_refs = refs[2:2 + TOP_K]
    (wsg_hbm, wsu_hbm, wsd_hbm, g2_ref, b2_ref, outp_ref, outs_ref,
     wsg, wsu, wsd, stage_a, stage_b, sem) = refs[2 + TOP_K:]

    @pl.when(pl.program_id(0) == 0)
    def _():
        _load_bf16(wsg_hbm, wsg, stage_a, sem, stage_a.shape[0])
        _load_bf16(wsu_hbm, wsu, stage_a, sem, stage_a.shape[0])
        _load_bf16(wsd_hbm, wsd, stage_b, sem, stage_b.shape[0])

    x1 = x1_ref[...]
    xb = x1.astype(BF16)
    hact = _silu(_dot(xb, wsg[...])) * _dot(xb, wsu[...])
    f = _dot(hact.astype(BF16), wsd[...])
    gates = gate_ref[...]
    half_d = x1.shape[1] // 2
    routed_hi = jnp.zeros((x1.shape[0], half_d), F32)
    routed_lo = jnp.zeros((x1.shape[0], half_d), F32)
    for kk in range(TOP_K):
        hi, lo = _unpack_bf16_pair(y_refs[kk][...])
        gk = gates[:, kk:kk + 1]
        routed_hi = routed_hi + hi * gk
        routed_lo = routed_lo + lo * gk
    routed = jnp.concatenate([routed_hi, routed_lo], axis=1)
    out = _layernorm(DEEPNORM_ALPHA * x1 + (f + routed), g2_ref[...], b2_ref[...])

    @pl.when(pl.program_id(0) < ntp)
    def _():
        outp_ref[...] = out

    @pl.when(pl.program_id(0) >= ntp)
    def _():
        outs_ref[...] = out


def _final(x1, gates, ycomb, w_sg, w_su, w_sd, g2, b2, t_prompt, t):
    tm = ROW_TILE
    assert t % tm == 0 and t_prompt % tm == 0
    nt = t // tm
    ntp = t_prompt // tm
    ff = w_sg.shape[1]
    const = lambda shape: pl.BlockSpec(shape, lambda i: tuple(0 for _ in shape))
    yspec = lambda kk: pl.BlockSpec((tm, D_MODEL // 2), lambda i: (kk * nt + i, 0))
    anyspec = pl.BlockSpec(memory_space=pl.ANY)
    return pl.pallas_call(
        functools.partial(_final_kernel, ntp),
        grid=(nt,),
        in_specs=[pl.BlockSpec((tm, D_MODEL), lambda i: (i, 0)),
                  pl.BlockSpec((tm, LANES), lambda i: (i, 0))]
                 + [yspec(kk) for kk in range(TOP_K)]
                 + [anyspec, anyspec, anyspec, const((1, D_MODEL)), const((1, D_MODEL))],
        out_specs=[pl.BlockSpec((tm, D_MODEL), lambda i: (jnp.minimum(i, ntp - 1), 0)),
                   pl.BlockSpec((tm, D_MODEL), lambda i: (jnp.maximum(i - ntp, 0), 0))],
        out_shape=[jax.ShapeDtypeStruct((t_prompt, D_MODEL), F32),
                   jax.ShapeDtypeStruct((t - t_prompt, D_MODEL), F32)],
        scratch_shapes=[pltpu.VMEM((D_MODEL, ff), BF16),
                        pltpu.VMEM((D_MODEL, ff), BF16),
                        pltpu.VMEM((ff, D_MODEL), BF16),
                        pltpu.VMEM((512, ff), F32),
                        pltpu.VMEM((128, D_MODEL), F32),
                        pltpu.SemaphoreType.DMA(())],
        compiler_params=_params(("arbitrary",)),
        name="shared_combine_ln",
    )(x1, gates, *([ycomb] * TOP_K), w_sg, w_su, w_sd, g2, b2)


def kernel(x_prompt, x_sample, cache_k_win, cache_v_win, state_gla, w_in, attn_sinks, attn_norm_g, w_gla_a2,
           b_gla_a, gla_norm_g, w_out, ln1_g, ln1_b, w_router, router_bias, w_exp_gate, w_exp_up, w_exp_down,
           w_sh_gate, w_sh_up, w_sh_down, ln2_g, ln2_b):
    assert w_in.shape[0] == DEPTH == 1 and x_prompt.shape[0] == 1 and x_sample.shape[1] == 1
    t = x_prompt.shape[1]
    bsz = x_sample.shape[0]
    win = cache_k_win.shape[2]
    assert t % WINDOW == 0 and win == WINDOW

    wag = jnp.pad(w_in[0][:, MAIN_COLS:], ((0, 0), (0, LANES - GLA_RANK)))
    w_in0 = w_in[0].astype(BF16)
    wa2 = jnp.pad(w_gla_a2[0], ((0, LANES - GLA_RANK), (0, 0)))
    ba = b_gla_a[0].reshape(1, GLA_QK_COLS)
    sinks = jnp.pad(attn_sinks[0].reshape(1, N_HEADS), ((0, 0), (0, LANES - N_HEADS)))
    ag = attn_norm_g[0].reshape(1, Q_COLS)
    gn = gla_norm_g[0].reshape(1, GLA_WIDTH)

    xp = x_prompt[0]
    hp, lap = _inproj(xp, w_in0, wag, wa2, ba)
    cp, sap, sbp = _rope_tables(jnp.arange(t, dtype=jnp.int32))
    oa_p, nk_p, nv_p = _swa_prompt(hp, cp, sap, sbp, attn_sinks[0], ag)
    og_p, s_p = _gla_prompt(hp, lap, gn)

    xs = x_sample[:, 0]
    hs, las = _inproj(xs, w_in0, wag, wa2, ba)
    cs, sas, sbs = _rope_tables(PAST_LEN + jnp.arange(1, dtype=jnp.int32))
    lane_head = jnp.arange(Q_COLS, dtype=jnp.int32)[:, None] // HEAD_DIM
    emat = (lane_head == jnp.arange(LANES, dtype=jnp.int32)[None, :]).astype(BF16)
    hs3 = hs.reshape(bsz, 1, MAIN_COLS)
    oa_s, nk_s, nv_s = _swa_decode(hs3, cache_k_win[0].reshape(bsz, win, KV_COLS),
                                   cache_v_win[0].reshape(bsz, win, KV_COLS), cs, sas, sbs, sinks, ag,
                                   emat, emat.T)
    og_s, s_s = _gla_decode(hs3, las.reshape(bsz, 1, GLA_QK_COLS), state_gla[0], gn)

    t_all = t + bsz
    x1, x1p, eidx, gates, counts = _outproj_router(
        xp, xs, oa_p, oa_s.reshape(bsz, Q_COLS), og_p, og_s.reshape(bsz, GLA_WIDTH), w_out[0],
        ln1_g[0].reshape(1, D_MODEL), ln1_b[0].reshape(1, D_MODEL), w_router[0], router_bias[0].reshape(1, -1))
    plan = _dispatch_plan(eidx[:t_all, :TOP_K], counts)
    ycomb = _moe_experts(x1p, plan, w_exp_gate[0], w_exp_up[0], w_exp_down[0], TOP_K * t_all)
    y_p, y_s = _final(x1, gates, ycomb, w_sh_gate[0], w_sh_up[0], w_sh_down[0],
                      ln2_g[0].reshape(1, D_MODEL), ln2_b[0].reshape(1, D_MODEL), t, t_all)

    y_prompt = y_p.reshape(1, t, D_MODEL)
    y_sample = y_s.reshape(bsz, 1, D_MODEL)
    new_k_prompt = nk_p.reshape(1, 1, win, N_KV_HEADS, HEAD_DIM)
    new_v_prompt = nv_p.reshape(1, 1, win, N_KV_HEADS, HEAD_DIM)
    new_s_prompt = s_p.reshape(1, 1, GLA_HEADS, GLA_DK, GLA_DV)
    new_k_sample = nk_s.reshape(1, bsz, win, N_KV_HEADS, HEAD_DIM)
    new_v_sample = nv_s.reshape(1, bsz, win, N_KV_HEADS, HEAD_DIM)
    new_s_sample = s_s.reshape(1, bsz, GLA_HEADS, GLA_DK, GLA_DV)
    return (y_prompt, y_sample, new_k_prompt, new_v_prompt, new_s_prompt,
            new_k_sample, new_v_sample, new_s_sample)
```

```python
import functools

import jax
import jax.numpy as jnp
from jax import lax
from jax.experimental import pallas as pl
from jax.experimental.pallas import tpu as pltpu

F32 = jnp.float32
BF16 = jnp.bfloat16

D_MODEL = 2048
PAST_LEN = 16384
HEAD_DIM = 64
N_HEADS = 16
N_KV_HEADS = 4
WINDOW = 128
ROPE_DIM = 16
ROPE_THETA = 500000.0
GLA_HEADS = 4
GLA_DV = 256
GLA_DK = 128
GLA_RANK = 16
GLA_TAU = 16.0
Q_COLS = N_HEADS * HEAD_DIM
KV_COLS = N_KV_HEADS * HEAD_DIM
GLA_QK_COLS = GLA_HEADS * GLA_DK
GLA_WIDTH = GLA_HEADS * GLA_DV
MAIN_COLS = Q_COLS + 2 * KV_COLS + 2 * GLA_QK_COLS + 2 * GLA_WIDTH
TOP_K = 8
N_GROUPS = 8
TOPK_GROUPS = 4
ROUTED_SCALE = 2.5
DEPTH = 1
DEEPNORM_ALPHA = (2 * DEPTH) ** 0.25
LN_EPS = 1e-5
RMS_EPS = 1e-6

LANES = 128
COL_BLOCK = 512
GLA_CHUNK = 64
GLA_SUB = 16
EXP_CLAMP = 80.0
ROW_TILE = 128
OUT_TILE = 256
MOE_BLK = 128
MOE_RING = 3
VMEM_LIMIT = 56 * 1024 * 1024


def _pick(n, cands):
    for c in cands:
        if n % c == 0:
            return c
    raise ValueError(f"no tile in {cands} divides {n}")


def _params(sem):
    return pltpu.CompilerParams(dimension_semantics=sem, vmem_limit_bytes=VMEM_LIMIT)


def _dot(a, b):
    return jnp.dot(a, b, preferred_element_type=F32)


def _dot_nt(a, b):
    return lax.dot_general(a, b, (((1,), (1,)), ((), ())), preferred_element_type=F32)


def _dot_tn(a, b):
    return lax.dot_general(a, b, (((0,), (0,)), ((), ())), preferred_element_type=F32)


def _split3(x):
    hi = x.astype(BF16)
    r1 = x - hi.astype(F32)
    mid = r1.astype(BF16)
    lo = (r1 - mid.astype(F32)).astype(BF16)
    return hi, mid, lo


def _pack_bf16_pair(a, b):
    hi = pltpu.bitcast(a.astype(BF16).astype(F32), jnp.uint32)
    lo = pltpu.bitcast(b.astype(BF16).astype(F32), jnp.uint32)
    return hi | lax.shift_right_logical(lo, jnp.uint32(16))


def _unpack_bf16_pair(w):
    hi = pltpu.bitcast(w & jnp.uint32(0xFFFF0000), F32)
    lo = pltpu.bitcast(lax.shift_left(w, jnp.uint32(16)), F32)
    return hi, lo


def _tile_lanes(t, width):
    reps = width // t.shape[-1]
    return t if reps == 1 else jnp.concatenate([t] * reps, axis=-1)


def _rope(x, c, sa, sb):
    w = x.shape[-1]
    return (x * _tile_lanes(c, w) + pltpu.roll(x, 8, 1) * _tile_lanes(sa, w)
            + pltpu.roll(x, w - 8, 1) * _tile_lanes(sb, w))


def _rope_tables(pos):
    half = ROPE_DIM // 2
    inv = 1.0 / (ROPE_THETA ** (jnp.arange(0, ROPE_DIM, 2, dtype=F32) / ROPE_DIM))
    ang = pos.astype(F32)[:, None] * inv[None, :]
    cos, sin = jnp.cos(ang), jnp.sin(ang)
    n = pos.shape[0]
    rest = HEAD_DIM - ROPE_DIM
    c = jnp.concatenate([cos, cos, jnp.ones((n, rest), F32)], axis=1)
    sa = jnp.concatenate([jnp.zeros((n, half), F32), sin, jnp.zeros((n, rest), F32)], axis=1)
    sb = jnp.concatenate([-sin, jnp.zeros((n, half + rest), F32)], axis=1)
    rep = LANES // HEAD_DIM
    return jnp.tile(c, (1, rep)), jnp.tile(sa, (1, rep)), jnp.tile(sb, (1, rep))


def _inproj_kernel(x_ref, w_ref, wag_ref, wa2_ref, ba_ref, h_ref, la_ref, xb_ref):
    @pl.when(pl.program_id(1) == 0)
    def _():
        xb = x_ref[...].astype(BF16)
        xb_ref[...] = xb
        ag = _dot(xb, wag_ref[...].astype(BF16))
        z = _dot(ag.astype(BF16), wa2_ref[...].astype(BF16)) + ba_ref[...]
        la_ref[...] = (jnp.minimum(z, 0.0) - jnp.log1p(jnp.exp(-jnp.abs(z)))) * (1.0 / GLA_TAU)

    h_ref[...] = _dot(xb_ref[...], w_ref[...])


def _inproj(x, w_in, wag, wa2, ba):
    t = x.shape[0]
    tm = _pick(t, (1024, 512, 256, 128))
    ncol = MAIN_COLS // COL_BLOCK
    return pl.pallas_call(
        _inproj_kernel,
        grid=(t // tm, ncol),
        in_specs=[
            pl.BlockSpec((tm, D_MODEL), lambda i, j: (i, 0)),
            pl.BlockSpec((D_MODEL, COL_BLOCK), lambda i, j: (0, j)),
            pl.BlockSpec((D_MODEL, LANES), lambda i, j: (0, 0)),
            pl.BlockSpec((LANES, GLA_QK_COLS), lambda i, j: (0, 0)),
            pl.BlockSpec((1, GLA_QK_COLS), lambda i, j: (0, 0)),
        ],
        out_specs=[
            pl.BlockSpec((tm, COL_BLOCK), lambda i, j: (i, j)),
            pl.BlockSpec((tm, GLA_QK_COLS), lambda i, j: (i, 0)),
        ],
        out_shape=[
            jax.ShapeDtypeStruct((t, MAIN_COLS), F32),
            jax.ShapeDtypeStruct((t, GLA_QK_COLS), F32),
        ],
        scratch_shapes=[pltpu.VMEM((tm, D_MODEL), BF16)],
        compiler_params=_params(("arbitrary", "arbitrary")),
        name="inproj",
    )(x, w_in, wag, wa2, ba)


def _half_variants(x):
    lane = lax.broadcasted_iota(jnp.int32, (x.shape[0], LANES), 1)
    lo = lane < HEAD_DIM
    out = []
    for g in range(N_KV_HEADS):
        chunk = x[:, (g // 2) * LANES:(g // 2 + 1) * LANES]
        swapped = pltpu.roll(chunk, HEAD_DIM, 1)
        u = g % 2
        for b in range(2):
            src = chunk if b == u else swapped
            keep = lo if b == 0 else jnp.logical_not(lo)
            out.append(jnp.where(keep, src, 0.0).astype(BF16))
    return out


def _swa_kernel(q_ref, kv_ref, c_ref, sa_ref, sb_ref, sink_ref, g_ref,
                oa_ref, nk_ref, nv_ref, kp_ref, vp_ref):
    i = pl.program_id(0)
    blk = q_ref.shape[0]
    c, sa, sb = c_ref[...], sa_ref[...], sb_ref[...]

    @pl.when(i == 0)
    def _():
        kp_ref[...] = jnp.zeros_like(kp_ref)
        vp_ref[...] = jnp.zeros_like(vp_ref)

    q = (_rope(q_ref[...], c, sa, sb) * (HEAD_DIM ** -0.5)).astype(BF16)
    kv = kv_ref[...]
    k = _rope(kv[:, :KV_COLS], c, sa, sb)
    v = kv[:, KV_COLS:]
    kc = _half_variants(k)
    vc = _half_variants(v)

    row = lax.broadcasted_iota(jnp.int32, (blk, 2 * blk), 0)
    col = lax.broadcasted_iota(jnp.int32, (blk, 2 * blk), 1)
    first_key = jnp.where(i > 0, 0, blk)
    mask = (col > row) & (col <= row + WINDOW) & (col >= first_key)

    pairs = []
    for p in range(N_HEADS // 2):
        qp = q[:, p * LANES:(p + 1) * LANES]
        acc = jnp.zeros((blk, LANES), F32)
        for b in range(2):
            h = 2 * p + b
            g = h // (N_HEADS // N_KV_HEADS)
            kcat = jnp.concatenate([kp_ref[2 * g + b], kc[2 * g + b]], axis=0)
            vcat = jnp.concatenate([vp_ref[2 * g + b], vc[2 * g + b]], axis=0)
            s = jnp.where(mask, _dot_nt(qp, kcat), -jnp.inf)
            sk = sink_ref[h]
            m = jnp.maximum(jnp.max(s, axis=1, keepdims=True), sk)
            e = jnp.exp(s - m)
            den = jnp.sum(e, axis=1, keepdims=True) + jnp.exp(sk - m)
            acc = acc + _dot((e / den).astype(BF16), vcat)
        pairs.append(acc)
    o = jnp.concatenate(pairs, axis=1)
    ms = jnp.mean(o * o, axis=1, keepdims=True)
    oa_ref[...] = (o * lax.rsqrt(ms + RMS_EPS) * g_ref[...]).astype(oa_ref.dtype)

    for n in range(2 * N_KV_HEADS):
        kp_ref[n] = kc[n]
        vp_ref[n] = vc[n]

    @pl.when(i == pl.num_programs(0) - 1)
    def _():
        nk_ref[...] = k
        nv_ref[...] = v


def _swa_prompt(h, c, sa, sb, sinks, g):
    t = h.shape[0]
    blk = WINDOW
    nb = t // blk
    return pl.pallas_call(
        _swa_kernel,
        grid=(nb,),
        in_specs=[
            pl.BlockSpec((blk, Q_COLS), lambda i: (i, 0)),
            pl.BlockSpec((blk, 2 * KV_COLS), lambda i: (i, Q_COLS // (2 * KV_COLS))),
            pl.BlockSpec((blk, LANES), lambda i: (i, 0)),
            pl.BlockSpec((blk, LANES), lambda i: (i, 0)),
            pl.BlockSpec((blk, LANES), lambda i: (i, 0)),
            pl.BlockSpec(memory_space=pltpu.SMEM),
            pl.BlockSpec((1, Q_COLS), lambda i: (0, 0)),
        ],
        out_specs=[
            pl.BlockSpec((blk, Q_COLS), lambda i: (i, 0)),
            pl.BlockSpec((blk, KV_COLS), lambda i: (0, 0)),
            pl.BlockSpec((blk, KV_COLS), lambda i: (0, 0)),
        ],
        out_shape=[
            jax.ShapeDtypeStruct((t, Q_COLS), BF16),
            jax.ShapeDtypeStruct((blk, KV_COLS), F32),
            jax.ShapeDtypeStruct((blk, KV_COLS), F32),
        ],
        scratch_shapes=[pltpu.VMEM((2 * N_KV_HEADS, blk, LANES), BF16),
                        pltpu.VMEM((2 * N_KV_HEADS, blk, LANES), BF16)],
        compiler_params=_params(("arbitrary",)),
        name="swa_prompt",
    )(h, h, c, sa, sb, sinks, g)


def _silu(x):
    return x * jax.nn.sigmoid(x)


def _gla_kernel(qg_ref, kg_ref, v1_ref, v2_ref, r1_ref, r2_ref, la_ref, gn_ref,
                og_ref, sfin_ref, s_ref):
    i = pl.program_id(0)
    rows = qg_ref.shape[0]
    ch = GLA_CHUNK
    nsub = ch // GLA_SUB

    @pl.when(i == 0)
    def _():
        s_ref[...] = jnp.zeros_like(s_ref)

    rr = lax.broadcasted_iota(jnp.int32, (ch, ch), 0)
    cc = lax.broadcasted_iota(jnp.int32, (ch, ch), 1)
    tri = jnp.where(rr >= cc, 1.0, 0.0).astype(BF16)
    pre = jnp.where(cc < (rr // GLA_SUB) * GLA_SUB, 1.0, 0.0).astype(BF16)
    mr = lax.broadcasted_iota(jnp.int32, (ch, nsub * ch), 0)
    mc = lax.broadcasted_iota(jnp.int32, (ch, nsub * ch), 1)
    amask = ((mc // ch) == (mr // GLA_SUB)) & ((mc % ch) <= mr)
    ones_t = jnp.ones((ch, LANES), BF16)

    def chunk(ci, carry):
        r0 = pl.multiple_of(ci * ch, ch)
        hi, mid, lo = _split3(la_ref[pl.ds(r0, ch), :])
        b_all = _dot(tri, hi) + _dot(tri, mid) + _dot(tri, lo)
        bs_all = _dot(pre, hi) + _dot(pre, mid) + _dot(pre, lo)
        bl_col_all = _dot_tn(hi, ones_t) + _dot_tn(mid, ones_t) + _dot_tn(lo, ones_t)
        for hh in range(GLA_HEADS):
            dk = slice(hh * GLA_DK, (hh + 1) * GLA_DK)
            vref, rref = (v1_ref, r1_ref) if hh < 2 else (v2_ref, r2_ref)
            dv = slice((hh % 2) * GLA_DV, (hh % 2 + 1) * GLA_DV)
            q = qg_ref[pl.ds(r0, ch), dk] * (GLA_DK ** -0.5)
            k = kg_ref[pl.ds(r0, ch), dk]
            v = vref[pl.ds(r0, ch), dv]
            r = rref[pl.ds(r0, ch), dv]
            b = b_all[:, dk]
            bs = bs_all[:, dk]
            blast = b[ch - 1:ch, :]
            bl_col = bl_col_all[hh * GLA_DK:(hh + 1) * GLA_DK, :]
            dec_col = jnp.exp(jnp.concatenate([bl_col, bl_col], axis=1))

            qt = (q * jnp.exp(b - bs)).astype(BF16)
            kparts = []
            for si in range(nsub):
                ref_row = bs[si * GLA_SUB:si * GLA_SUB + 1, :]
                kparts.append((k * jnp.exp(jnp.minimum(ref_row - b, EXP_CLAMP))).astype(BF16))
            kcat = jnp.concatenate(kparts, axis=0)
            a = jnp.where(amask, _dot_nt(qt, kcat), 0.0).astype(BF16)
            vb = v.astype(BF16)
            vrep = jnp.concatenate([vb] * nsub, axis=0)
            s_old = s_ref[hh]
            o = _dot(a, vrep) + _dot((q * jnp.exp(b)).astype(BF16), s_old.astype(BF16))
            kd = (k * jnp.exp(blast - b)).astype(BF16)
            s_ref[hh] = dec_col * s_old + _dot_tn(kd, vb)

            ms = jnp.mean(o * o, axis=1, keepdims=True)
            gn = gn_ref[:, hh * GLA_DV:(hh + 1) * GLA_DV]
            y = o * lax.rsqrt(ms + RMS_EPS) * gn
            og_ref[pl.ds(r0, ch), hh * GLA_DV:(hh + 1) * GLA_DV] = (y * _silu(r)).astype(og_ref.dtype)
        return carry

    lax.fori_loop(0, rows // ch, chunk, 0)

    @pl.when(i == pl.num_programs(0) - 1)
    def _():
        sfin_ref[...] = s_ref[...]


def _gla_prompt(h, la, gn):
    t = h.shape[0]
    rows = _pick(t, (512, 256, 128, 64))
    cb = COL_BLOCK
    base = (Q_COLS + 2 * KV_COLS) // cb
    spec = lambda j: pl.BlockSpec((rows, cb), lambda i: (i, j))
    return pl.pallas_call(
        _gla_kernel,
        grid=(t // rows,),
        in_specs=[spec(base), spec(base + 1), spec(base + 2), spec(base + 3), spec(base + 4), spec(base + 5),
                  pl.BlockSpec((rows, GLA_QK_COLS), lambda i: (i, 0)),
                  pl.BlockSpec((1, GLA_WIDTH), lambda i: (0, 0))],
        out_specs=[
            pl.BlockSpec((rows, GLA_WIDTH), lambda i: (i, 0)),
            pl.BlockSpec((GLA_HEADS, GLA_DK, GLA_DV), lambda i: (0, 0, 0)),
        ],
        out_shape=[
            jax.ShapeDtypeStruct((t, GLA_WIDTH), BF16),
            jax.ShapeDtypeStruct((GLA_HEADS, GLA_DK, GLA_DV), F32),
        ],
        scratch_shapes=[pltpu.VMEM((GLA_HEADS, GLA_DK, GLA_DV), F32)],
        compiler_params=_params(("arbitrary",)),
        name="gla_prompt",
    )(h, h, h, h, h, h, la, gn)


def _dup_heads(x):
    lane = lax.broadcasted_iota(jnp.int32, (x.shape[0], LANES), 1)
    lo = lane < HEAD_DIM
    out = []
    for g in range(N_KV_HEADS):
        chunk = x[:, (g // 2) * LANES:(g // 2 + 1) * LANES]
        swapped = pltpu.roll(chunk, HEAD_DIM, 1)
        out.append(jnp.where(lo, chunk, swapped) if g % 2 == 0 else jnp.where(lo, swapped, chunk))
    return out


def _swa_dec_kernel(q_ref, kv_ref, ck_ref, cv_ref, c_ref, sa_ref, sb_ref, sink_ref, g_ref, e_ref, et_ref,
                    oa_ref, nk_ref, nv_ref):
    win = ck_ref.shape[1]
    c, sa, sb = c_ref[...], sa_ref[...], sb_ref[...]
    q8 = jnp.broadcast_to(q_ref[0], (8, Q_COLS))
    kv8 = jnp.broadcast_to(kv_ref[0], (8, 2 * KV_COLS))
    q = _rope(q8, c, sa, sb) * (HEAD_DIM ** -0.5)
    knew = _rope(kv8[:, :KV_COLS], c, sa, sb)
    vnew = kv8[:, KV_COLS:]

    row = lax.broadcasted_iota(jnp.int32, (win, KV_COLS), 0)
    last = row == win - 1
    keys = jnp.where(last, jnp.broadcast_to(knew[0:1], (win, KV_COLS)), pltpu.roll(ck_ref[0], win - 1, 0))
    vals = jnp.where(last, jnp.broadcast_to(vnew[0:1], (win, KV_COLS)), pltpu.roll(cv_ref[0], win - 1, 0))
    nk_ref[0] = keys
    nv_ref[0] = vals

    qb = q.astype(BF16).astype(F32)
    kd = _dup_heads(keys.astype(BF16).astype(F32))
    vd = _dup_heads(vals.astype(BF16).astype(F32))
    npair = N_HEADS // 2
    prod = jnp.concatenate(
        [kd[p // 2] * jnp.broadcast_to(qb[0:1, p * LANES:(p + 1) * LANES], (win, LANES)) for p in range(npair)],
        axis=1)
    st = _dot(prod.astype(BF16), e_ref[...])
    sk = sink_ref[...]
    m = jnp.maximum(jnp.max(st, axis=0, keepdims=True), sk)
    e = jnp.exp(st - m)
    den = jnp.sum(e, axis=0, keepdims=True) + jnp.exp(sk - m)
    pe = _dot((e / den).astype(BF16), et_ref[...])
    vcat = jnp.concatenate([vd[p // 2] for p in range(npair)], axis=1)
    o = jnp.sum(pe * vcat, axis=0, keepdims=True)
    ms = jnp.mean(o * o, axis=1, keepdims=True)
    oa_ref[0] = (o * lax.rsqrt(ms + RMS_EPS) * g_ref[...]).astype(oa_ref.dtype)


def _swa_decode(h3, ck, cv, c, sa, sb, sinks, g, e, et):
    bsz = h3.shape[0]
    win = ck.shape[1]
    const = lambda shape: pl.BlockSpec(shape, lambda b: tuple(0 for _ in shape))
    return pl.pallas_call(
        _swa_dec_kernel,
        grid=(bsz,),
        in_specs=[
            pl.BlockSpec((1, 1, Q_COLS), lambda b: (b, 0, 0)),
            pl.BlockSpec((1, 1, 2 * KV_COLS), lambda b: (b, 0, Q_COLS // (2 * KV_COLS))),
            pl.BlockSpec((1, win, KV_COLS), lambda b: (b, 0, 0)),
            pl.BlockSpec((1, win, KV_COLS), lambda b: (b, 0, 0)),
            const((1, LANES)), const((1, LANES)), const((1, LANES)), const((1, LANES)),
            const((1, Q_COLS)), const((Q_COLS, LANES)), const((LANES, Q_COLS)),
        ],
        out_specs=[
            pl.BlockSpec((1, 1, Q_COLS), lambda b: (b, 0, 0)),
            pl.BlockSpec((1, win, KV_COLS), lambda b: (b, 0, 0)),
            pl.BlockSpec((1, win, KV_COLS), lambda b: (b, 0, 0)),
        ],
        out_shape=[
            jax.ShapeDtypeStruct((bsz, 1, Q_COLS), BF16),
            jax.ShapeDtypeStruct((bsz, win, KV_COLS), F32),
            jax.ShapeDtypeStruct((bsz, win, KV_COLS), F32),
        ],
        compiler_params=_params(("arbitrary",)),
        name="swa_decode",
    )(h3, h3, ck, cv, c, sa, sb, sinks, g, e, et)


def _rows8(rows):
    w = rows[0].shape[1]
    ridx = lax.broadcasted_iota(jnp.int32, (8, w), 0)
    out = jnp.zeros((8, w), F32)
    for n, r in enumerate(rows):
        out = jnp.where(ridx == n, jnp.broadcast_to(r, (8, w)), out)
    return out


def _gla_dec_kernel(qg_ref, kg_ref, v1_ref, v2_ref, r1_ref, r2_ref, la_ref, s_ref, gn_ref, og_ref, sn_ref):
    ones8 = jnp.where(lax.broadcasted_iota(jnp.int32, (8, GLA_DV), 0) < 3, 1.0, 0.0).astype(BF16)
    outs = []
    for hh in range(GLA_HEADS):
        dk = slice(hh * GLA_DK, (hh + 1) * GLA_DK)
        vref, rref = (v1_ref, r1_ref) if hh < 2 else (v2_ref, r2_ref)
        dv = slice((hh % 2) * GLA_DV, (hh % 2 + 1) * GLA_DV)
        q = qg_ref[0][:, dk] * (GLA_DK ** -0.5)
        k = kg_ref[0][:, dk]
        la = la_ref[0][:, dk]
        v = vref[0][:, dv]
        r = rref[0][:, dv]
        s_old = s_ref[0, hh]
        dec = jnp.exp(la)
        qb = q.astype(BF16).astype(F32)
        kb = k.astype(BF16).astype(F32)
        score = jnp.sum(qb * kb, axis=1, keepdims=True)
        q8 = _rows8([q * dec]).astype(BF16)
        o = score * v + _dot(q8, s_old.astype(BF16))[0:1, :]
        dhi, dmid, dlo = _split3(dec)
        d8 = _rows8([dhi.astype(F32), dmid.astype(F32), dlo.astype(F32)]).astype(BF16)
        dec_col = _dot_tn(d8, ones8)
        k8 = _rows8([k]).astype(BF16)
        v8 = _rows8([v]).astype(BF16)
        sn_ref[0, hh] = dec_col * s_old + _dot_tn(k8, v8)
        ms = jnp.mean(o * o, axis=1, keepdims=True)
        gn = gn_ref[:, hh * GLA_DV:(hh + 1) * GLA_DV]
        outs.append(o * lax.rsqrt(ms + RMS_EPS) * gn * _silu(r))
    og_ref[0] = jnp.concatenate(outs, axis=1).astype(og_ref.dtype)


def _gla_decode(h3, la3, s0, gn):
    bsz = h3.shape[0]
    cb = COL_BLOCK
    base = (Q_COLS + 2 * KV_COLS) // cb
    spec = lambda j: pl.BlockSpec((1, 1, cb), lambda b: (b, 0, j))
    return pl.pallas_call(
        _gla_dec_kernel,
        grid=(bsz,),
        in_specs=[spec(base), spec(base + 1), spec(base + 2), spec(base + 3), spec(base + 4), spec(base + 5),
                  pl.BlockSpec((1, 1, GLA_QK_COLS), lambda b: (b, 0, 0)),
                  pl.BlockSpec((1, GLA_HEADS, GLA_DK, GLA_DV), lambda b: (b, 0, 0, 0)),
                  pl.BlockSpec((1, GLA_WIDTH), lambda b: (0, 0))],
        out_specs=[
            pl.BlockSpec((1, 1, GLA_WIDTH), lambda b: (b, 0, 0)),
            pl.BlockSpec((1, GLA_HEADS, GLA_DK, GLA_DV), lambda b: (b, 0, 0, 0)),
        ],
        out_shape=[
            jax.ShapeDtypeStruct((bsz, 1, GLA_WIDTH), BF16),
            jax.ShapeDtypeStruct(s0.shape, s0.dtype),
        ],
        compiler_params=_params(("arbitrary",)),
        name="gla_decode",
    )(h3, h3, h3, h3, h3, h3, la3, s0, gn)


def _layernorm(v, g, b):
    mu = jnp.mean(v, axis=1, keepdims=True)
    d = v - mu
    var = jnp.mean(d * d, axis=1, keepdims=True)
    return d * lax.rsqrt(var + LN_EPS) * g + b


def _load_bf16(w_hbm, dst_ref, stage_ref, sem, chunk):
    for s in range(w_hbm.shape[0] // chunk):
        cp = pltpu.make_async_copy(w_hbm.at[pl.ds(s * chunk, chunk), :], stage_ref, sem)
        cp.start()
        cp.wait()
        dst_ref[pl.ds(s * chunk, chunk), :] = stage_ref[...].astype(BF16)


def _route(logits, bias, valid):
    rows, ne = logits.shape
    per = ne // N_GROUPS
    ninf = -jnp.inf
    lane = lax.broadcasted_iota(jnp.int32, (rows, ne), 1)
    gid = lane // per
    scores = jax.nn.sigmoid(logits)
    biased = scores + bias
    gfull = jnp.zeros((rows, ne), F32)
    gcols = []
    for g in range(N_GROUPS):
        ing = gid == g
        xg = jnp.where(ing, biased, ninf)
        m1 = jnp.max(xg, axis=1, keepdims=True)
        cnt = jnp.sum(jnp.where(xg == m1, 1.0, 0.0), axis=1, keepdims=True)
        m2 = jnp.max(jnp.where(xg < m1, xg, ninf), axis=1, keepdims=True)
        gs = m1 + jnp.where(cnt >= 2.0, m1, m2)
        gcols.append(gs)
        gfull = jnp.where(ing, gs, gfull)
    rank = jnp.zeros((rows, ne), jnp.int32)
    for g in range(N_GROUPS):
        beats = (gcols[g] > gfull) | ((gcols[g] == gfull) & (gid > g))
        rank = rank + jnp.where(beats, 1, 0)
    masked = jnp.where(rank < TOPK_GROUPS, biased, ninf)

    lane_f = lane.astype(F32)
    out_lane = lax.broadcasted_iota(jnp.int32, (rows, LANES), 1)
    eout = jnp.zeros((rows, LANES), F32)
    gout = jnp.zeros((rows, LANES), F32)
    total = jnp.zeros((rows, 1), F32)
    chosen = jnp.zeros((rows, ne), F32)
    for kk in range(TOP_K):
        m = jnp.max(masked, axis=1, keepdims=True)
        idx = jnp.min(jnp.where(masked == m, lane_f, float(ne)), axis=1, keepdims=True)
        hit = lane_f == idx
        gk = jnp.sum(jnp.where(hit, scores, 0.0), axis=1, keepdims=True)
        masked = jnp.where(hit, ninf, masked)
        chosen = jnp.where(hit, 1.0, chosen)
        total = total + gk
        eout = jnp.where(out_lane == kk, idx, eout)
        gout = jnp.where(out_lane == kk, gk, gout)
    counts = jnp.sum(jnp.where(valid, chosen, 0.0), axis=0, keepdims=True)
    return eout.astype(jnp.int32), gout / total * ROUTED_SCALE, counts


def _outproj_kernel(ntp, t_valid, xp_ref, xs_ref, oap_ref, oas_ref, ogp_ref, ogs_ref, wout_hbm, g1_ref, b1_ref,
                    wr_ref, rb_ref, x1_ref, x1p_ref, eidx_ref, gate_ref, cnt_ref, wo_ref, stage_ref, sem):
    @pl.when(pl.program_id(0) == 0)
    def _():
        _load_bf16(wout_hbm, wo_ref, stage_ref, sem, stage_ref.shape[0])
        cnt_ref[...] = jnp.zeros_like(cnt_ref)

    is_prompt = pl.program_id(0) < ntp
    x = jnp.where(is_prompt, xp_ref[...], xs_ref[...])
    oa = jnp.where(is_prompt, oap_ref[...], oas_ref[...])
    og = jnp.where(is_prompt, ogp_ref[...], ogs_ref[...])
    half = oa.shape[1]
    mix = _dot(oa, wo_ref[pl.ds(0, half), :]) + _dot(og, wo_ref[pl.ds(half, half), :])
    x1 = _layernorm(DEEPNORM_ALPHA * x + mix, g1_ref[...], b1_ref[...])
    x1_ref[...] = x1
    x1p_ref[...] = _pack_bf16_pair(x1[:, :D_MODEL // 2], x1[:, D_MODEL // 2:])
    logits = _dot(x1.astype(BF16), wr_ref[...].astype(BF16))
    tm = x1.shape[0]
    row = pl.program_id(0) * tm + lax.broadcasted_iota(jnp.int32, (tm, 1), 0)
    eidx, gates, counts = _route(logits, rb_ref[...], row < t_valid)
    eidx_ref[...] = eidx
    gate_ref[...] = gates
    cnt_ref[...] += counts


def _outproj_router(xp, xs, oap, oas, ogp, ogs, w_out, g1, b1, w_router, rbias):
    tm = OUT_TILE
    tp, ts = xp.shape[0], xs.shape[0]
    assert tp % tm == 0
    ntp = tp // tm
    pad = -ts % tm
    xs, oas, ogs = (jnp.pad(a, ((0, pad), (0, 0))) for a in (xs, oas, ogs))
    t = tp + ts + pad
    ne = w_router.shape[1]
    const = lambda shape: pl.BlockSpec(shape, lambda i: tuple(0 for _ in shape))
    prompt = lambda width: pl.BlockSpec((tm, width), lambda i: (jnp.minimum(i, ntp - 1), 0))
    sample = lambda width: pl.BlockSpec((tm, width), lambda i: (jnp.maximum(i - ntp, 0), 0))
    return pl.pallas_call(
        functools.partial(_outproj_kernel, ntp, tp + ts),
        grid=(t // tm,),
        in_specs=[
            prompt(D_MODEL), sample(D_MODEL), prompt(Q_COLS), sample(Q_COLS), prompt(GLA_WIDTH), sample(GLA_WIDTH),
            pl.BlockSpec(memory_space=pl.ANY),
            const((1, D_MODEL)), const((1, D_MODEL)), const((D_MODEL, ne)), const((1, ne)),
        ],
        out_specs=[
            pl.BlockSpec((tm, D_MODEL), lambda i: (i, 0)),
            pl.BlockSpec((tm, D_MODEL // 2), lambda i: (i, 0)),
            pl.BlockSpec((tm, LANES), lambda i: (i, 0)),
            pl.BlockSpec((tm, LANES), lambda i: (i, 0)),
            const((1, ne)),
        ],
        out_shape=[
            jax.ShapeDtypeStruct((t, D_MODEL), F32),
            jax.ShapeDtypeStruct((t, D_MODEL // 2), jnp.uint32),
            jax.ShapeDtypeStruct((t, LANES), jnp.int32),
            jax.ShapeDtypeStruct((t, LANES), F32),
            jax.ShapeDtypeStruct((1, ne), F32),
        ],
        scratch_shapes=[pltpu.VMEM((D_MODEL, D_MODEL), BF16),
                        pltpu.VMEM((512, D_MODEL), F32),
                        pltpu.SemaphoreType.DMA(())],
        compiler_params=_params(("arbitrary",)),
        name="outproj_router",
    )(xp, xs, oap, oas, ogp, ogs, w_out, g1, b1, w_router, rbias)


FLAG_NEW_EXPERT = 1
FLAG_HAS_NEXT_EXPERT = 2
FLAG_WEIGHT_SLOT = 4


def _moe_kernel(bw0, bn, wexp, wlo, whi, wflag, wnext,
                src_hbm, dst_hbm, x_hbm, wg_hbm, wu_hbm, wd_hbm, y_hbm,
                xbuf, ybuf, xbb, wgf, wuf, wdf, wgb, wub, wdb, srcs, dsts, gsem, ssem, isem, wsem):
    nblk = src_hbm.shape[0]
    ring, blk = xbuf.shape[:2]
    ff = wgb.shape[1]

    def slot_of(j):
        return j % ring if isinstance(j, int) else lax.rem(j, ring)

    def src_dma(j, q):
        j = min(j, nblk - 1) if isinstance(j, int) else jnp.minimum(j, nblk - 1)
        return pltpu.make_async_copy(src_hbm.at[j], srcs.at[q], isem.at[0, q])

    def dst_dma(j, q):
        return pltpu.make_async_copy(dst_hbm.at[j], dsts.at[q], isem.at[1, q])

    def gather_copy(tok, r, q):
        return pltpu.make_async_copy(x_hbm.at[pl.ds(tok, 1), :], xbuf.at[q, pl.ds(r, 1), :], gsem.at[q])

    def scatter_copy(dst, r, q):
        return pltpu.make_async_copy(ybuf.at[q, pl.ds(r, 1), :], y_hbm.at[pl.ds(dst, 1), :], ssem.at[q])

    def issue_gather(q, rows=None):
        for r in (range(blk) if rows is None else rows):
            gather_copy(srcs[q, r], r, q).start()

    def wait_gather(q):
        for r in range(blk):
            gather_copy(0, r, q).wait()

    def issue_scatter(q, rows=None):
        for r in (range(blk) if rows is None else rows):
            scatter_copy(dsts[q, r], r, q).start(priority=r % 2)

    def wait_scatter(q):
        for r in range(blk):
            scatter_copy(0, r, q).wait()

    def weight_copies(e, ws):
        return (pltpu.make_async_copy(wg_hbm.at[e], wgf.at[ws], wsem.at[ws, 0]),
                pltpu.make_async_copy(wu_hbm.at[e], wuf.at[ws], wsem.at[ws, 1]),
                pltpu.make_async_copy(wd_hbm.at[e], wdf.at[ws], wsem.at[ws, 2]))

    def start_weights(e, ws):
        for cp in weight_copies(e, ws):
            cp.start(priority=1)

    def cast_weights(ws):
        step = 64
        for c in range(0, wgb.shape[0], step):
            wgb[pl.ds(c, step), :] = wgf[ws, pl.ds(c, step), :].astype(BF16)
            wub[pl.ds(c, step), :] = wuf[ws, pl.ds(c, step), :].astype(BF16)
        step = 16
        for c in range(0, wdb.shape[0], step):
            wdb[pl.ds(c, step), :] = wdf[ws, pl.ds(c, step), :].astype(BF16)

    def item(w, q, first, hooks=None):
        fl = wflag[w]
        anchor = hooks is not None
        hooks = hooks or (lambda: None,) * 4

        @pl.when((fl & FLAG_NEW_EXPERT) != 0)
        def _():
            ws = jnp.where((fl & FLAG_WEIGHT_SLOT) != 0, 1, 0)
            for cp in weight_copies(0, ws):
                cp.wait()
            cast_weights(ws)

            @pl.when((fl & FLAG_HAS_NEXT_EXPERT) != 0)
            def _():
                start_weights(wnext[w], 1 - ws)

        def after_ring_dmas():
            probe = xbuf[q, pl.ds(0, 8), pl.ds(0, LANES)]
            zero = lax.shift_right_logical(lax.shift_right_logical(probe, jnp.uint32(16)), jnp.uint32(16))
            return pltpu.bitcast(zero, F32)[0:1, 0:1]

        hooks[0]()
        gate = _dot(xbb[...], wgb[...]) + (after_ring_dmas() if anchor else 0.0)
        hooks[1]()
        up = _dot(xbb[...], wub[...]) + (after_ring_dmas() if anchor else 0.0)
        hooks[2]()
        hact = _silu(gate) * up
        rows = lax.broadcasted_iota(jnp.int32, (blk, ff), 0)
        hact = jnp.where((rows >= wlo[w]) & (rows < whi[w]), hact, 0.0)
        y = _dot(hact.astype(BF16), wdb[...])
        hooks[3]()
        half_d = y.shape[1] // 2
        if first:
            ybuf[q] = _pack_bf16_pair(y[:, :half_d], y[:, half_d:])
        else:
            hi, lo = _unpack_bf16_pair(ybuf[q])
            ybuf[q] = _pack_bf16_pair(hi + y[:, :half_d], lo + y[:, half_d:])

    def step(b):
        static = isinstance(b, int)
        q = slot_of(b)
        wait_gather(q)
        if static:
            if b >= ring:
                wait_scatter(q)
        else:
            @pl.when(b >= ring)
            def _():
                wait_scatter(q)
        if not (static and b == 0):
            dst_dma(b - 1, slot_of(b + 2)).wait()
        src_dma(b + 2, slot_of(b + 2)).wait()
        src_dma(b + 3, q).start()
        dst_dma(b, q).start()
        hi, lo = _unpack_bf16_pair(xbuf[q])
        half_d = xbb.shape[1] // 2
        xbb[:, pl.ds(0, half_d)] = hi.astype(BF16)
        xbb[:, pl.ds(half_d, half_d)] = lo.astype(BF16)
        w0 = bw0[b]

        qn = slot_of(b + 2)
        half = blk // 2
        hooks = [lambda: issue_gather(qn, range(0, half)), lambda: issue_gather(qn, range(half, blk))]
        if static and b == 0:
            hooks += [lambda: None, lambda: None]
        else:
            hooks += [lambda: issue_scatter(qn, range(0, half)), lambda: issue_scatter(qn, range(half, blk))]
        item(w0, q, True, hooks)

        def extra(k, carry):
            item(w0 + k, q, False)
            return carry
        lax.fori_loop(1, bn[b], extra, 0)

    assert nblk >= 2 and ring == 3
    for j in range(3):
        src_dma(j, j).start()
    start_weights(wexp[0], 0)
    for j in range(2):
        src_dma(j, j).wait()
        issue_gather(j)

    step(0)
    lax.fori_loop(1, nblk, lambda b, c: (step(b), c)[1], 0)

    last = nblk - 1
    dst_dma(last, last % ring).wait()
    issue_scatter(last % ring)
    src_dma(last + 3, last % ring).wait()
    wait_gather((last + 1) % ring)
    wait_gather((last + 2) % ring)
    for j in range(max(0, nblk - ring), nblk):
        wait_scatter(j % ring)


def _moe_experts(x1p, plan, w_gate, w_up, w_down, n_out_rows):
    src, dst = plan[-2:]
    ne, d, ff = w_gate.shape
    assert x1p.shape[1] * 2 == d and x1p.dtype == jnp.uint32
    anyspec = pl.BlockSpec(memory_space=pl.ANY)
    grid_spec = pltpu.PrefetchScalarGridSpec(
        num_scalar_prefetch=7,
        grid=(1,),
        in_specs=[anyspec] * 6,
        out_specs=anyspec,
        scratch_shapes=[
            pltpu.VMEM((MOE_RING, MOE_BLK, d // 2), jnp.uint32),
            pltpu.VMEM((MOE_RING, MOE_BLK, d // 2), jnp.uint32),
            pltpu.VMEM((MOE_BLK, d), BF16),
            pltpu.VMEM((2, d, ff), F32),
            pltpu.VMEM((2, d, ff), F32),
            pltpu.VMEM((2, ff, d), F32),
            pltpu.VMEM((d, ff), BF16),
            pltpu.VMEM((d, ff), BF16),
            pltpu.VMEM((ff, d), BF16),
            pltpu.SMEM((MOE_RING, MOE_BLK), jnp.int32),
            pltpu.SMEM((MOE_RING, MOE_BLK), jnp.int32),
            pltpu.SemaphoreType.DMA((MOE_RING,)),
            pltpu.SemaphoreType.DMA((MOE_RING,)),
            pltpu.SemaphoreType.DMA((2, MOE_RING)),
            pltpu.SemaphoreType.DMA((2, 3)),
        ],
    )
    return pl.pallas_call(
        _moe_kernel,
        grid_spec=grid_spec,
        out_shape=jax.ShapeDtypeStruct((n_out_rows, d // 2), jnp.uint32),
        compiler_params=_params(("arbitrary",)),
        name="moe_experts",
    )(*plan[:-2], src, dst, x1p, w_gate, w_up, w_down)


def _dispatch_plan(eidx, counts_f):
    t = eidx.shape[0]
    ne = counts_f.shape[-1]
    n = t * TOP_K
    blk = MOE_BLK
    assert n % blk == 0
    nblk = n // blk
    nwmax = nblk + ne - 1
    bits = max(n - 1, 1).bit_length()
    flat = jnp.arange(n, dtype=jnp.int32)
    skeys = jnp.sort(eidx.reshape(-1) * (1 << bits) + flat)
    order = skeys & ((1 << bits) - 1)
    tok = order // TOP_K
    src = tok.reshape(nblk, blk)
    dst = ((order % TOP_K) * t + tok).reshape(nblk, blk)

    counts = counts_f.reshape(ne).astype(jnp.int32)
    ends = jnp.cumsum(counts)
    starts = ends - counts
    fb = starts // blk
    nit = jnp.where(counts > 0, (ends + blk - 1) // blk - fb, 0)
    icum = jnp.cumsum(nit)
    nwork = icum[-1]
    eids = jnp.arange(ne, dtype=jnp.int32)
    nxt = jnp.flip(lax.cummin(jnp.flip(jnp.where(counts > 0, eids, ne)), axis=0))
    nxt_after = jnp.concatenate([nxt[1:], jnp.full((1,), ne, jnp.int32)])
    used_rank = jnp.cumsum((counts > 0).astype(jnp.int32)) - 1

    w = jnp.arange(nwmax, dtype=jnp.int32)
    onehot = (w[:, None] >= (icum - nit)[None, :]) & (w[:, None] < icum[None, :])
    pick = lambda v: jnp.sum(jnp.where(onehot, v[None, :], 0), axis=1)
    we = pick(eids)
    wb = pick(fb) + w - pick(icum - nit)
    lo = jnp.clip(pick(starts) - wb * blk, 0, blk)
    hi = jnp.clip(pick(ends) - wb * blk, 0, blk)
    wn = pick(nxt_after)
    wb = jnp.where(w < nwork, wb, nblk)
    prev_e = jnp.concatenate([jnp.full((1,), -1, jnp.int32), we[:-1]])
    flags = (jnp.where(we != prev_e, FLAG_NEW_EXPERT, 0)
             + jnp.where(wn < ne, FLAG_HAS_NEXT_EXPERT, 0)
             + (pick(used_rank) % 2) * FLAG_WEIGHT_SLOT)
    wn = jnp.minimum(wn, ne - 1)
    in_block = wb[None, :] == jnp.arange(nblk, dtype=jnp.int32)[:, None]
    bn = jnp.sum(in_block.astype(jnp.int32), axis=1)
    bw0 = jnp.cumsum(bn) - bn
    i32 = lambda v: v.astype(jnp.int32)
    return (i32(bw0), i32(bn), i32(we), i32(lo), i32(hi), i32(flags), i32(wn), src, dst)


def _final_kernel(ntp, *refs):
    x1_ref, gate_ref = refs[0], refs[1]
    y_refs = refs[2:2 + TOP_K]
    (wsg_hbm, wsu_hbm, wsd_hbm, g2_ref, b2_ref, outp_ref, outs_ref,
     wsg, wsu, wsd, stage_a, stage_b, sem) = refs[2 + TOP_K:]

    @pl.when(pl.program_id(0) == 0)
    def _():
        _load_bf16(wsg_hbm, wsg, stage_a, sem, stage_a.shape[0])
        _load_bf16(wsu_hbm, wsu, stage_a, sem, stage_a.shape[0])
        _load_bf16(wsd_hbm, wsd, stage_b, sem, stage_b.shape[0])

    x1 = x1_ref[...]
    xb = x1.astype(BF16)
    hact = _silu(_dot(xb, wsg[...])) * _dot(xb, wsu[...])
    f = _dot(hact.astype(BF16), wsd[...])
    gates = gate_ref[...]
    half_d = x1.shape[1] // 2
    routed_hi = jnp.zeros((x1.shape[0], half_d), F32)
    routed_lo = jnp.zeros((x1.shape[0], half_d), F32)
    for kk in range(TOP_K):
        hi, lo = _unpack_bf16_pair(y_refs[kk][...])
        gk = gates[:, kk:kk + 1]
        routed_hi = routed_hi + hi * gk
        routed_lo = routed_lo + lo * gk
    routed = jnp.concatenate([routed_hi, routed_lo], axis=1)
    out = _layernorm(DEEPNORM_ALPHA * x1 + (f + routed), g2_ref[...], b2_ref[...])

    @pl.when(pl.program_id(0) < ntp)
    def _():
        outp_ref[...] = out

    @pl.when(pl.program_id(0) >= ntp)
    def _():
        outs_ref[...] = out


def _final(x1, gates, ycomb, w_sg, w_su, w_sd, g2, b2, t_prompt, t):
    tm = ROW_TILE
    assert t % tm == 0 and t_prompt % tm == 0
    nt = t // tm
    ntp = t_prompt // tm
    ff = w_sg.shape[1]
    const = lambda shape: pl.BlockSpec(shape, lambda i: tuple(0 for _ in shape))
    yspec = lambda kk: pl.BlockSpec((tm, D_MODEL // 2), lambda i: (kk * nt + i, 0))
    anyspec = pl.BlockSpec(memory_space=pl.ANY)
    return pl.pallas_call(
        functools.partial(_final_kernel, ntp),
        grid=(nt,),
        in_specs=[pl.BlockSpec((tm, D_MODEL), lambda i: (i, 0)),
                  pl.BlockSpec((tm, LANES), lambda i: (i, 0))]
                 + [yspec(kk) for kk in range(TOP_K)]
                 + [anyspec, anyspec, anyspec, const((1, D_MODEL)), const((1, D_MODEL))],
        out_specs=[pl.BlockSpec((tm, D_MODEL), lambda i: (jnp.minimum(i, ntp - 1), 0)),
                   pl.BlockSpec((tm, D_MODEL), lambda i: (jnp.maximum(i - ntp, 0), 0))],
        out_shape=[jax.ShapeDtypeStruct((t_prompt, D_MODEL), F32),
                   jax.ShapeDtypeStruct((t - t_prompt, D_MODEL), F32)],
        scratch_shapes=[pltpu.VMEM((D_MODEL, ff), BF16),
                        pltpu.VMEM((D_MODEL, ff), BF16),
                        pltpu.VMEM((ff, D_MODEL), BF16),
                        pltpu.VMEM((512, ff), F32),
                        pltpu.VMEM((128, D_MODEL), F32),
                        pltpu.SemaphoreType.DMA(())],
        compiler_params=_params(("arbitrary",)),
        name="shared_combine_ln",
    )(x1, gates, *([ycomb] * TOP_K), w_sg, w_su, w_sd, g2, b2)


def kernel(x_prompt, x_sample, cache_k_win, cache_v_win, state_gla, w_in, attn_sinks, attn_norm_g, w_gla_a2,
           b_gla_a, gla_norm_g, w_out, ln1_g, ln1_b, w_router, router_bias, w_exp_gate, w_exp_up, w_exp_down,
           w_sh_gate, w_sh_up, w_sh_down, ln2_g, ln2_b):
    assert w_in.shape[0] == DEPTH == 1 and x_prompt.shape[0] == 1 and x_sample.shape[1] == 1
    t = x_prompt.shape[1]
    bsz = x_sample.shape[0]
    win = cache_k_win.shape[2]
    assert t % WINDOW == 0 and win == WINDOW

    wag = jnp.pad(w_in[0][:, MAIN_COLS:], ((0, 0), (0, LANES - GLA_RANK)))
    w_in0 = w_in[0].astype(BF16)
    wa2 = jnp.pad(w_gla_a2[0], ((0, LANES - GLA_RANK), (0, 0)))
    ba = b_gla_a[0].reshape(1, GLA_QK_COLS)
    sinks = jnp.pad(attn_sinks[0].reshape(1, N_HEADS), ((0, 0), (0, LANES - N_HEADS)))
    ag = attn_norm_g[0].reshape(1, Q_COLS)
    gn = gla_norm_g[0].reshape(1, GLA_WIDTH)

    xp = x_prompt[0]
    hp, lap = _inproj(xp, w_in0, wag, wa2, ba)
    cp, sap, sbp = _rope_tables(jnp.arange(t, dtype=jnp.int32))
    oa_p, nk_p, nv_p = _swa_prompt(hp, cp, sap, sbp, attn_sinks[0], ag)
    og_p, s_p = _gla_prompt(hp, lap, gn)

    xs = x_sample[:, 0]
    hs, las = _inproj(xs, w_in0, wag, wa2, ba)
    cs, sas, sbs = _rope_tables(PAST_LEN + jnp.arange(1, dtype=jnp.int32))
    lane_head = jnp.arange(Q_COLS, dtype=jnp.int32)[:, None] // HEAD_DIM
    emat = (lane_head == jnp.arange(LANES, dtype=jnp.int32)[None, :]).astype(BF16)
    hs3 = hs.reshape(bsz, 1, MAIN_COLS)
    oa_s, nk_s, nv_s = _swa_decode(hs3, cache_k_win[0].reshape(bsz, win, KV_COLS),
                                   cache_v_win[0].reshape(bsz, win, KV_COLS), cs, sas, sbs, sinks, ag,
                                   emat, emat.T)
    og_s, s_s = _gla_decode(hs3, las.reshape(bsz, 1, GLA_QK_COLS), state_gla[0], gn)

    t_all = t + bsz
    x1, x1p, eidx, gates, counts = _outproj_router(
        xp, xs, oa_p, oa_s.reshape(bsz, Q_COLS), og_p, og_s.reshape(bsz, GLA_WIDTH), w_out[0],
        ln1_g[0].reshape(1, D_MODEL), ln1_b[0].reshape(1, D_MODEL), w_router[0], router_bias[0].reshape(1, -1))
    plan = _dispatch_plan(eidx[:t_all, :TOP_K], counts)
    ycomb = _moe_experts(x1p, plan, w_exp_gate[0], w_exp_up[0], w_exp_down[0], TOP_K * t_all)
    y_p, y_s = _final(x1, gates, ycomb, w_sh_gate[0], w_sh_up[0], w_sh_down[0],
                      ln2_g[0].reshape(1, D_MODEL), ln2_b[0].reshape(1, D_MODEL), t, t_all)

    y_prompt = y_p.reshape(1, t, D_MODEL)
    y_sample = y_s.reshape(bsz, 1, D_MODEL)
    new_k_prompt = nk_p.reshape(1, 1, win, N_KV_HEADS, HEAD_DIM)
    new_v_prompt = nv_p.reshape(1, 1, win, N_KV_HEADS, HEAD_DIM)
    new_s_prompt = s_p.reshape(1, 1, GLA_HEADS, GLA_DK, GLA_DV)
    new_k_sample = nk_s.reshape(1, bsz, win, N_KV_HEADS, HEAD_DIM)
    new_v_sample = nv_s.reshape(1, bsz, win, N_KV_HEADS, HEAD_DIM)
    new_s_sample = s_s.reshape(1, bsz, GLA_HEADS, GLA_DK, GLA_DV)
    return (y_prompt, y_sample, new_k_prompt, new_v_prompt, new_s_prompt,
            new_k_sample, new_v_sample, new_s_sample)
```
